```python
import math, functools
import jax, jax.numpy as jnp
from jax import lax
import numpy as np

D_MODEL = 2048
BATCH = 2
SEQ = 4096
DEPTH = 4
DEC_BATCH = 8
DEC_SEQ = 2048
PAST_LEN = 128

HEAD_DIM = 128
A_GROUPS = ((128, 1), (512, 4), (2048, 16))
A_HEADS_PER_GROUP = 4
A_HEADS = A_HEADS_PER_GROUP * len(A_GROUPS)
A_OUT = A_HEADS_PER_GROUP * HEAD_DIM
B_HEADS = 4
B_QK_DIM = 64
B_V_DIM = 2 * B_QK_DIM
ROPE_THETA = 500000.0
ROPE_FRACTION = 4
Q_BLOCK = 128
D_FF_DENSE = 5632
N_EXPERTS = 8
TOP_K = 2
D_FF_EXPERT = 7168
MOE_ROW_BLOCK = 256
N_DENSE_LAYERS = (DEPTH + 1) // 2
N_MOE_LAYERS = DEPTH // 2
ALPHA = (2 * DEPTH) ** 0.25
BETA = (8 * DEPTH) ** -0.25
LN_EPS = 1e-5
MASK_VALUE = -1e30

A_QKV = A_HEADS * HEAD_DIM
B_QK = B_HEADS * 2 * B_QK_DIM
B_V = B_HEADS * B_V_DIM
COL_SIZES = (A_QKV, A_QKV, A_QKV, B_QK, B_QK, B_V, D_MODEL, D_MODEL)
IN_COLS = sum(COL_SIZES)
SPLIT_POINTS = tuple(int(c) for c in np.cumsum(COL_SIZES)[:-1])

kernel_name = 'hybrid_dilated_diff_encoder'


def layer_norm(x, g, b):
    xf = x.astype(jnp.float32)
    mu = xf.mean(-1, keepdims=True)
    var = jnp.square(xf - mu).mean(-1, keepdims=True)
    return ((xf - mu) * lax.rsqrt(var + LN_EPS) * g + b).astype(x.dtype)


def partial_rope(x, rot_dim):
    S = x.shape[1]
    half = rot_dim // 2
    inv_freq = ROPE_THETA ** (-jnp.arange(half, dtype=jnp.float32) / half)
    ang = jnp.arange(S, dtype=jnp.float32)[:, None] * inv_freq[None, :]
    cos = jnp.cos(ang)[None, :, None, :]
    sin = jnp.sin(ang)[None, :, None, :]
    xr = x[..., :rot_dim].astype(jnp.float32)
    x1, x2 = xr[..., :half], xr[..., half:]
    rot = jnp.concatenate([x1 * cos - x2 * sin, x2 * cos + x1 * sin], axis=-1).astype(x.dtype)
    return jnp.concatenate([rot, x[..., rot_dim:]], axis=-1)


def dilated_window_attention(q, k, v, n_side, dilation):
    B, S, H, dh = q.shape
    L = S // dilation
    band = n_side
    nb = -(-L // band)
    Lp = nb * band

    def split(t):
        return t.reshape(B, L, dilation, H, dh).transpose(0, 2, 1, 3, 4)

    qs = split(q * (dh ** -0.5))
    qb = jnp.pad(qs, ((0, 0), (0, 0), (0, Lp - L), (0, 0), (0, 0))).reshape(B, dilation, nb, band, H, dh)

    def key_bands(t):
        tp = jnp.pad(split(t), ((0, 0), (0, 0), (band, Lp - L + band), (0, 0), (0, 0)))
        tp = tp.reshape(B, dilation, nb + 2, band, H, dh)
        return jnp.concatenate([tp[:, :, :-2], tp[:, :, 1:-1], tp[:, :, 2:]], axis=3)

    kb, vb = key_bands(k), key_bands(v)
    s = jnp.einsum('brnqhd,brnkhd->brnhqk', qb, kb, preferred_element_type=jnp.float32)
    blk = jnp.arange(nb)[:, None, None]
    qpos = blk * band + jnp.arange(band)[None, :, None]
    kpos = blk * band - band + jnp.arange(3 * band)[None, None, :]
    mask = (jnp.abs(kpos - qpos) <= n_side) & (kpos >= 0) & (kpos < L)
    s = jnp.where(mask[None, None, :, None], s, MASK_VALUE)
    m = s.max(-1, keepdims=True)
    p = jnp.exp(s - m)
    den = p.sum(-1)
    o = jnp.einsum('brnhqk,brnkhd->brnqhd', p, vb.astype(jnp.float32))
    o = o / jnp.swapaxes(den, -1, -2)[..., None]
    lse = jnp.swapaxes(m[..., 0] + jnp.log(den), -1, -2)
    o = o.reshape(B, dilation, Lp, H, dh)[:, :, :L].transpose(0, 2, 1, 3, 4).reshape(B, S, H, dh)
    lse = lse.reshape(B, dilation, Lp, H)[:, :, :L].transpose(0, 2, 1, 3).reshape(B, S, H)
    return o, lse


def differential_attention(q, k, v, lam):
    B, S, H, _, dq = q.shape
    nq = S // Q_BLOCK
    qblk = jnp.moveaxis((q * (dq ** -0.5)).reshape(B, nq, Q_BLOCK, H, 2, dq), 1, 0)
    vf = v.astype(jnp.float32)

    def attend(qb):
        s = jnp.einsum('bqhcd,bkhcd->bhcqk', qb, k, preferred_element_type=jnp.float32)
        p = jax.nn.softmax(s, axis=-1)
        a = p[:, :, 0] - lam * p[:, :, 1]
        return jnp.einsum('bhqk,bkhd->bqhd', a, vf)

    o = lax.map(attend, qblk)
    return jnp.moveaxis(o, 0, 1).reshape(B, S, H, v.shape[-1])


def token_mixer(h, w_in, w_branch_a, w_branch_b, w_out, lam_vecs, subln_g, lam_init):
    B, S, _ = h.shape
    proj = h @ w_in
    qa, ka, va, qb, kb, vb, ga, gb = jnp.split(proj, SPLIT_POINTS, axis=-1)

    qa = partial_rope(qa.reshape(B, S, A_HEADS, HEAD_DIM), HEAD_DIM // ROPE_FRACTION)
    ka = partial_rope(ka.reshape(B, S, A_HEADS, HEAD_DIM), HEAD_DIM // ROPE_FRACTION)
    va = va.reshape(B, S, A_HEADS, HEAD_DIM)
    outs, lses = [], []
    for g, (window, dilation) in enumerate(A_GROUPS):
        sl = slice(g * A_HEADS_PER_GROUP, (g + 1) * A_HEADS_PER_GROUP)
        o, lse = dilated_window_attention(qa[:, :, sl], ka[:, :, sl], va[:, :, sl], window // (2 * dilation), dilation)
        outs.append(o)
        lses.append(lse)
    wts = jax.nn.softmax(jnp.stack(lses, axis=0), axis=0)
    oa = jnp.sum(wts[..., None] * jnp.stack(outs, axis=0), axis=0)
    ya = oa.reshape(B, S, A_OUT).astype(h.dtype) @ w_branch_a

    qb = partial_rope(qb.reshape(B, S, 2 * B_HEADS, B_QK_DIM), B_QK_DIM // ROPE_FRACTION).reshape(B, S, B_HEADS, 2, B_QK_DIM)
    kb = partial_rope(kb.reshape(B, S, 2 * B_HEADS, B_QK_DIM), B_QK_DIM // ROPE_FRACTION).reshape(B, S, B_HEADS, 2, B_QK_DIM)
    vb = vb.reshape(B, S, B_HEADS, B_V_DIM)
    lv = lam_vecs.astype(jnp.float32)
    lam = jnp.exp(jnp.sum(lv[0] * lv[1])) - jnp.exp(jnp.sum(lv[2] * lv[3])) + lam_init
    ob = differential_attention(qb, kb, vb, lam)
    ob = ob * lax.rsqrt(jnp.mean(jnp.square(ob), axis=-1, keepdims=True) + LN_EPS) * subln_g * (1.0 - lam_init)
    yb = ob.reshape(B, S, B_V).astype(h.dtype) @ w_branch_b

    mix = jax.nn.sigmoid(ga) * ya + jax.nn.sigmoid(gb) * yb
    return mix @ w_out


def swiglu(h, w_gate, w_up, w_down):
    return (jax.nn.silu(h @ w_gate) * (h @ w_up)) @ w_down


def moe_swiglu(h, w_router, w_gate, w_up, w_down):
    T, D = h.shape
    logits = jnp.einsum('td,de->te', h, w_router, preferred_element_type=jnp.float32)
    top_logits, top_idx = lax.top_k(logits, TOP_K)
    gates = jax.nn.softmax(top_logits, axis=-1)
    n_assign = T * TOP_K
    flat_e = top_idx.reshape(n_assign)
    flat_tok = jnp.arange(n_assign, dtype=jnp.int32) // TOP_K
    order = jnp.argsort(flat_e)
    sorted_e = flat_e[order]
    counts = jnp.bincount(flat_e, length=N_EXPERTS)
    padded = (counts + MOE_ROW_BLOCK - 1) // MOE_ROW_BLOCK * MOE_ROW_BLOCK
    pad_end = jnp.cumsum(padded)
    pad_start = pad_end - padded
    start = jnp.cumsum(counts) - counts
    dest_sorted = pad_start[sorted_e] + jnp.arange(n_assign, dtype=jnp.int32) - start[sorted_e]
    n_blocks = -(-(n_assign + N_EXPERTS * (MOE_ROW_BLOCK - 1)) // MOE_ROW_BLOCK)
    n_rows = n_blocks * MOE_ROW_BLOCK
    row_tok = jnp.zeros((n_rows,), jnp.int32).at[dest_sorted].set(flat_tok[order])
    block_start = jnp.arange(n_blocks, dtype=jnp.int32) * MOE_ROW_BLOCK
    block_expert = jnp.minimum(jnp.searchsorted(pad_end, block_start, side='right'), N_EXPERTS - 1)
    xs = h[row_tok].reshape(n_blocks, MOE_ROW_BLOCK, D)

    def expert_block(args):
        xb, e = args
        return swiglu(xb, w_gate[e], w_up[e], w_down[e])

    ys = lax.map(expert_block, (xs, block_expert)).reshape(n_rows, D)
    dest = jnp.zeros((n_assign,), jnp.int32).at[order].set(dest_sorted)
    y = jnp.einsum('tkd,tk->td', ys[dest].reshape(T, TOP_K, D).astype(jnp.float32), gates)
    return y.astype(h.dtype)


def encoder_trunk(x, ln_in_g, ln_in_b, w_in, w_branch_a, w_branch_b, w_out, diff_lambda, diff_subln_g,
                  ln_mix_g, ln_mix_b, ln_ffn_g, ln_ffn_b, ffn_w_gate, ffn_w_up, ffn_w_down,
                  moe_router, moe_w_gate, moe_w_up, moe_w_down):
    B, S, D = x.shape
    x = layer_norm(x, ln_in_g, ln_in_b)
    for l in range(DEPTH):
        lam_init = 0.8 - 0.6 * math.exp(-0.3 * l)
        mix = token_mixer(x, w_in[l], w_branch_a[l], w_branch_b[l], w_out[l], diff_lambda[l], diff_subln_g[l], lam_init)
        x = layer_norm(ALPHA * x + mix, ln_mix_g[l], ln_mix_b[l])
        if l % 2 == 0:
            i = l // 2
            f = swiglu(x, ffn_w_gate[i], ffn_w_up[i], ffn_w_down[i])
        else:
            i = l // 2
            f = moe_swiglu(x.reshape(B * S, D), moe_router[i], moe_w_gate[i], moe_w_up[i], moe_w_down[i]).reshape(B, S, D)
        x = layer_norm(ALPHA * x + f, ln_ffn_g[l], ln_ffn_b[l])
    return x


def setup_inputs(seed: int = 0) -> dict:
    key = jax.random.key(seed)
    ks = jax.random.split(key, 24)
    f32 = jnp.float32

    def nrm(k, shape, scale):
        return jax.random.normal(k, shape, f32) * scale

    va0 = 2 * A_QKV
    vb0 = 3 * A_QKV + 2 * B_QK
    col_scale = jnp.ones((IN_COLS,), f32).at[va0:va0 + A_QKV].set(BETA).at[vb0:vb0 + B_V].set(BETA)
    return {
        'x_prompt': nrm(ks[0], (BATCH, SEQ, D_MODEL), 1.0),
        'x_sample': nrm(ks[1], (DEC_BATCH, DEC_SEQ, D_MODEL), 1.0),
        'ln_in_g': 1.0 + nrm(ks[2], (D_MODEL,), 0.02),
        'ln_in_b': nrm(ks[3], (D_MODEL,), 0.02),
        'w_in': nrm(ks[4], (DEPTH, D_MODEL, IN_COLS), D_MODEL ** -0.5) * col_scale,
        'w_branch_a': nrm(ks[5], (DEPTH, A_OUT, D_MODEL), A_OUT ** -0.5),
        'w_branch_b': nrm(ks[6], (DEPTH, B_V, D_MODEL), B_V ** -0.5),
        'w_out': nrm(ks[7], (DEPTH, D_MODEL, D_MODEL), BETA * D_MODEL ** -0.5),
        'diff_lambda': nrm(ks[8], (DEPTH, 4, B_QK_DIM), 0.1),
        'diff_subln_g': 1.0 + nrm(ks[9], (DEPTH, B_V_DIM), 0.02),
        'ln_mix_g': 1.0 + nrm(ks[10], (DEPTH, D_MODEL), 0.02),
        'ln_mix_b': nrm(ks[11], (DEPTH, D_MODEL), 0.02),
        'ln_ffn_g': 1.0 + nrm(ks[12], (DEPTH, D_MODEL), 0.02),
        'ln_ffn_b': nrm(ks[13], (DEPTH, D_MODEL), 0.02),
        'ffn_w_gate': nrm(ks[14], (N_DENSE_LAYERS, D_MODEL, D_FF_DENSE), D_MODEL ** -0.5),
        'ffn_w_up': nrm(ks[15], (N_DENSE_LAYERS, D_MODEL, D_FF_DENSE), D_MODEL ** -0.5),
        'ffn_w_down': nrm(ks[16], (N_DENSE_LAYERS, D_FF_DENSE, D_MODEL), BETA * D_FF_DENSE ** -0.5),
        'moe_router': nrm(ks[17], (N_MOE_LAYERS, D_MODEL, N_EXPERTS), D_MODEL ** -0.5),
        'moe_w_gate': nrm(ks[18], (N_MOE_LAYERS, N_EXPERTS, D_MODEL, D_FF_EXPERT), D_MODEL ** -0.5),
        'moe_w_up': nrm(ks[19], (N_MOE_LAYERS, N_EXPERTS, D_MODEL, D_FF_EXPERT), D_MODEL ** -0.5),
        'moe_w_down': nrm(ks[20], (N_MOE_LAYERS, N_EXPERTS, D_FF_EXPERT, D_MODEL), BETA * D_FF_EXPERT ** -0.5),
    }


def reference(x_prompt, x_sample, ln_in_g, ln_in_b, w_in, w_branch_a, w_branch_b, w_out, diff_lambda, diff_subln_g,
              ln_mix_g, ln_mix_b, ln_ffn_g, ln_ffn_b, ffn_w_gate, ffn_w_up, ffn_w_down,
              moe_router, moe_w_gate, moe_w_up, moe_w_down):
    weights = (ln_in_g, ln_in_b, w_in, w_branch_a, w_branch_b, w_out, diff_lambda, diff_subln_g,
               ln_mix_g, ln_mix_b, ln_ffn_g, ln_ffn_b, ffn_w_gate, ffn_w_up, ffn_w_down,
               moe_router, moe_w_gate, moe_w_up, moe_w_down)
    y_prompt = encoder_trunk(x_prompt, *weights)
    y_sample = encoder_trunk(x_sample, *weights)
    return (y_prompt, y_sample)
```

```python
import functools
import math

import jax
import jax.numpy as jnp
from jax import lax
from jax.experimental import pallas as pl
from jax.experimental.pallas import tpu as pltpu

F32 = jnp.float32
BF16 = jnp.bfloat16

HEAD_DIM = 128
A_GROUPS = ((128, 1), (512, 4), (2048, 16))
A_HEADS_PER_GROUP = 4
A_OUT = A_HEADS_PER_GROUP * HEAD_DIM
A_QKV = len(A_GROUPS) * A_OUT
B_HEADS = 4
B_QK_DIM = 64
B_V_DIM = 2 * B_QK_DIM
B_COLS = B_HEADS * B_V_DIM
ROPE_THETA = 500000.0
ROPE_FRACTION = 4
N_EXPERTS = 8
TOP_K = 2
LN_EPS = 1e-5
MASK_VALUE = -1e30
GATE_COL0 = 3 * A_QKV + 3 * B_COLS

LANES = 128
COL_BLOCK = 512
VMEM_LIMIT = 56 * 1024 * 1024


def _cparams(sem):
    return pltpu.CompilerParams(dimension_semantics=sem, vmem_limit_bytes=VMEM_LIMIT)


def _layer_norm(y, g, b):
    mu = jnp.mean(y, axis=-1, keepdims=True)
    yc = y - mu
    var = jnp.mean(yc * yc, axis=-1, keepdims=True)
    return yc * lax.rsqrt(var + LN_EPS) * g + b


def _dot(a, b):
    return jnp.dot(a, b, preferred_element_type=F32)


def _dot_nt(a, b):
    return lax.dot_general(a, b, (((1,), (1,)), ((), ())), preferred_element_type=F32)


def _ln_in_kernel(x_ref, g_ref, b_ref, xf_ref, xb_ref):
    y = _layer_norm(x_ref[...], g_ref[...], b_ref[...])
    xf_ref[...] = y
    xb_ref[...] = y.astype(BF16)


def _ln_in(x, g, b, tm):
    T, D = x.shape
    row = pl.BlockSpec((tm, D), lambda i: (i, 0))
    vec = pl.BlockSpec((1, D), lambda i: (0, 0))
    return pl.pallas_call(
        _ln_in_kernel,
        grid=(T // tm,),
        in_specs=[row, vec, vec],
        out_specs=[row, row],
        out_shape=[jax.ShapeDtypeStruct((T, D), F32), jax.ShapeDtypeStruct((T, D), BF16)],
        compiler_params=_cparams(("parallel",)),
    )(x, g.reshape(1, D), b.reshape(1, D))


def _rope_tables(s_max):
    pos = jnp.arange(s_max, dtype=F32)[:, None]

    def tables(head, n_rep):
        rot = head // ROPE_FRACTION
        half = rot // 2
        inv_freq = ROPE_THETA ** (-jnp.arange(half, dtype=F32) / half)
        ang = pos * inv_freq[None, :]
        cos, sin = jnp.cos(ang), jnp.sin(ang)
        zero = jnp.zeros((s_max, head - rot), F32)
        zh = jnp.zeros((s_max, half), F32)
        c = jnp.concatenate([cos, cos, jnp.ones((s_max, head - rot), F32)], axis=1)
        s_up = jnp.concatenate([-sin, zh, zero], axis=1)
        s_dn = jnp.concatenate([zh, sin, zero], axis=1)
        return [jnp.tile(t, (1, n_rep)) for t in (c, s_up, s_dn)]

    return jnp.stack(tables(HEAD_DIM, 1) + tables(B_QK_DIM, 2), axis=0)


def _proj_kernel(x_ref, w_ref, tab_ref, o_ref):
    j = pl.program_id(1)
    acc = _dot(x_ref[...], w_ref[...])
    n_a = A_QKV // COL_BLOCK
    is_a = j < 2 * n_a
    jb = j - 3 * n_a
    is_b = (jb == 0) | (jb == 1)

    def rope(t0, half, scale):
        for c in range(COL_BLOCK // LANES):
            sl = slice(c * LANES, (c + 1) * LANES)
            xc = acc[:, sl]
            up = pltpu.roll(xc, LANES - half, 1)
            dn = pltpu.roll(xc, half, 1)
            y = xc * tab_ref[t0] + up * tab_ref[t0 + 1] + dn * tab_ref[t0 + 2]
            o_ref[:, sl] = (y * scale).astype(BF16)

    @pl.when(is_a)
    def _():
        rope(0, HEAD_DIM // ROPE_FRACTION // 2, jnp.where(j < n_a, HEAD_DIM ** -0.5, 1.0).astype(F32))

    @pl.when(is_b)
    def _():
        rope(3, B_QK_DIM // ROPE_FRACTION // 2, jnp.where(jb == 0, B_QK_DIM ** -0.5, 1.0).astype(F32))

    @pl.when(jnp.logical_not(is_a | is_b))
    def _():
        o_ref[...] = acc.astype(BF16)


def _proj(xb, w, tab, seqs, tm):
    T, D = xb.shape
    C = w.shape[1]
    bounds = []
    t0 = 0
    for n_seq, S in seqs:
        assert S % tm == 0
        bounds.append((t0 // tm, S // tm))
        t0 += n_seq * S
    assert t0 == T

    def pos_block(i):
        blk = (i - bounds[0][0]) % bounds[0][1]
        for first, per in bounds[1:]:
            blk = jnp.where(i >= first, (i - first) % per, blk)
        return blk

    return pl.pallas_call(
        _proj_kernel,
        grid=(T // tm, C // COL_BLOCK),
        in_specs=[
            pl.BlockSpec((tm, D), lambda i, j: (i, 0)),
            pl.BlockSpec((D, COL_BLOCK), lambda i, j: (0, j)),
            pl.BlockSpec((6, tm, LANES), lambda i, j: (0, pos_block(i), 0)),
        ],
        out_specs=pl.BlockSpec((tm, COL_BLOCK), lambda i, j: (i, j)),
        out_shape=jax.ShapeDtypeStruct((T, C), BF16),
        compiler_params=_cparams(("parallel", "arbitrary")),
    )(xb, w, tab)


def _attn_a_kernel(q_ref, k_ref, v_ref, *rest, L, tq, kw, n_side):
    o_ref, lse_ref = rest[-2], rest[-1]
    n_q = L // tq

    def body(qi, carry):
        q0 = pl.multiple_of(qi * tq, tq)
        start = pl.multiple_of(jnp.clip(q0 - n_side, 0, L - kw), n_side)
        qpos = q0 + lax.broadcasted_iota(jnp.int32, (tq, 1), 0)
        kpos = start + lax.broadcasted_iota(jnp.int32, (1, kw), 1)
        mask = jnp.abs(kpos - qpos) <= n_side
        for h in range(A_HEADS_PER_GROUP):
            sl = slice(h * HEAD_DIM, (h + 1) * HEAD_DIM)
            q = q_ref[pl.ds(q0, tq), sl]
            k = k_ref[pl.ds(start, kw), sl]
            v = v_ref[pl.ds(start, kw), sl]
            s = jnp.where(mask, _dot_nt(q, k), MASK_VALUE)
            m = jnp.max(s, axis=-1, keepdims=True)
            p = jnp.exp(s - m)
            den = jnp.sum(p, axis=-1, keepdims=True)
            o = _dot(p.astype(BF16), v) / den
            o_ref[pl.ds(q0, tq), sl] = o.astype(BF16)
            lse_ref[pl.ds(q0, tq), sl] = jnp.broadcast_to(m + jnp.log(den), (tq, HEAD_DIM))
        return carry

    lax.fori_loop(0, n_q, body, 0)


def _attn_a_group(proj, prev, g, n_seq, S, row0, T):
    window, d = A_GROUPS[g]
    n_side = window // (2 * d)
    C = proj.shape[1]
    L = S // d
    tq = min(128, L)
    kw = min(L, tq + 2 * n_side)
    assert L % tq == 0 and tq % n_side == 0 and (L - kw) % n_side == 0
    cb = C // COL_BLOCK
    n_g = len(A_GROUPS)
    blk0 = row0 // S
    pv = proj.reshape(T // d, d * C)

    def spec(part):
        return pl.BlockSpec((L, COL_BLOCK), lambda b, r: (blk0 + b, r * cb + part * n_g + g))

    out_spec = pl.BlockSpec((L, A_OUT), lambda b, r: (blk0 + b, r))
    out_shape = [jax.ShapeDtypeStruct((T // d, d * A_OUT), BF16),
                 jax.ShapeDtypeStruct((T // d, d * A_OUT), F32)]
    if prev is None:
        prev = (jnp.zeros((T, A_OUT), BF16), jnp.zeros((T, A_OUT), F32))
    in_specs = [spec(0), spec(1), spec(2)] + [pl.BlockSpec(memory_space=pl.ANY)] * 2
    args = [pv, pv, pv, prev[0].reshape(T // d, d * A_OUT), prev[1].reshape(T // d, d * A_OUT)]
    aliases = {3: 0, 4: 1}
    o, lse = pl.pallas_call(
        functools.partial(_attn_a_kernel, L=L, tq=tq, kw=kw, n_side=n_side),
        grid=(n_seq, d),
        in_specs=in_specs,
        out_specs=[out_spec, out_spec],
        out_shape=out_shape,
        input_output_aliases=aliases,
        compiler_params=_cparams(("parallel", "parallel")),
    )(*args)
    return o.reshape(T, A_OUT), lse.reshape(T, A_OUT)


def _attn_b_kernel(q_ref, k_ref, v_ref, lam_ref, g_ref, c_ref, *rest):
    o_ref = rest[-1]
    q = q_ref[...]
    k = k_ref[...]
    v = v_ref[...]
    lane = lax.broadcasted_iota(jnp.int32, (1, 2 * B_QK_DIM), 1)
    zero = jnp.zeros_like(q)
    lam_init = c_ref[0:1, 0:1]
    lv = lam_ref[...]
    lam = (jnp.exp(jnp.sum(lv[0:1] * lv[1:2], axis=-1, keepdims=True))
           - jnp.exp(jnp.sum(lv[2:3] * lv[3:4], axis=-1, keepdims=True)) + lam_init)

    def softmax_v(qz):
        s = _dot_nt(qz, k)
        m = jnp.max(s, axis=-1, keepdims=True)
        p = jnp.exp(s - m)
        den = jnp.sum(p, axis=-1, keepdims=True)
        return _dot(p.astype(BF16), v) / den

    o = softmax_v(jnp.where(lane < B_QK_DIM, q, zero)) - lam * softmax_v(jnp.where(lane >= B_QK_DIM, q, zero))
    o = o * lax.rsqrt(jnp.mean(o * o, axis=-1, keepdims=True) + LN_EPS) * g_ref[...] * (1.0 - lam_init)
    o_ref[...] = o.astype(BF16)


def _attn_b(proj, prev, lam_vecs, subln_g, consts, n_seq, S, row0, T, tq):
    C = proj.shape[1]
    blk0 = row0 // S
    qb0 = row0 // tq
    n_qt = S // tq
    col = lambda part: (3 * A_QKV + part * B_COLS) // LANES
    in_specs = [
        pl.BlockSpec((tq, LANES), lambda b, h, i: (qb0 + b * n_qt + i, col(0) + h)),
        pl.BlockSpec((S, LANES), lambda b, h, i: (blk0 + b, col(1) + h)),
        pl.BlockSpec((S, LANES), lambda b, h, i: (blk0 + b, col(2) + h)),
        pl.BlockSpec((4, B_QK_DIM), lambda b, h, i: (0, 0)),
        pl.BlockSpec((1, B_V_DIM), lambda b, h, i: (0, 0)),
        pl.BlockSpec((1, LANES), lambda b, h, i: (0, 0)),
    ]
    if prev is None:
        prev = jnp.zeros((T, B_COLS), BF16)
    in_specs.append(pl.BlockSpec(memory_space=pl.ANY))
    args = [proj, proj, proj, lam_vecs, subln_g.reshape(1, B_V_DIM), consts, prev]
    aliases = {6: 0}
    return pl.pallas_call(
        _attn_b_kernel,
        grid=(n_seq, B_HEADS, n_qt),
        in_specs=in_specs,
        out_specs=pl.BlockSpec((tq, LANES), lambda b, h, i: (qb0 + b * n_qt + i, h)),
        out_shape=jax.ShapeDtypeStruct((T, B_COLS), BF16),
        input_output_aliases=aliases,
        compiler_params=_cparams(("parallel", "parallel", "arbitrary")),
    )(*args)


def _post_kernel(o1_ref, o2_ref, o3_ref, l1_ref, l2_ref, l3_ref, ob_ref, ga_ref, gb_ref, x_ref,
                 wa_ref, wb_ref, wo_ref, g_ref, b_ref, *rest, alpha, with_router):
    if with_router:
        wr_ref, xf_ref, xb_ref, lg_ref = rest
    else:
        xf_ref, xb_ref = rest
    l1, l2, l3 = l1_ref[...], l2_ref[...], l3_ref[...]
    m = jnp.maximum(jnp.maximum(l1, l2), l3)
    e1, e2, e3 = jnp.exp(l1 - m), jnp.exp(l2 - m), jnp.exp(l3 - m)
    oa = (e1 * o1_ref[...].astype(F32) + e2 * o2_ref[...].astype(F32) + e3 * o3_ref[...].astype(F32)) / (e1 + e2 + e3)
    ya = _dot(oa.astype(BF16), wa_ref[...])
    yb = _dot(ob_ref[...], wb_ref[...])
    mix = jax.nn.sigmoid(ga_ref[...].astype(F32)) * ya + jax.nn.sigmoid(gb_ref[...].astype(F32)) * yb
    z = _dot(mix.astype(BF16), wo_ref[...])
    y = _layer_norm(alpha * x_ref[...] + z, g_ref[...], b_ref[...])
    xf_ref[...] = y
    xb_ref[...] = y.astype(BF16)
    if with_router:
        lg_ref[...] = jnp.dot(y, wr_ref[...], preferred_element_type=F32, precision=lax.Precision.HIGHEST)


def _post(oa, ob, proj, x, wa, wb, wo, g, b, wr, alpha, tm):
    T, D = x.shape
    assert GATE_COL0 % D == 0
    gcol = GATE_COL0 // D
    row = lambda w: pl.BlockSpec((tm, w), lambda i: (i, 0))
    full = lambda a: pl.BlockSpec(a.shape, lambda i: (0,) * a.ndim)
    g2, b2 = g.reshape(1, D), b.reshape(1, D)
    (o1, l1), (o2, l2), (o3, l3) = oa
    in_specs = [row(A_OUT)] * 6 + [row(B_COLS),
                                   pl.BlockSpec((tm, D), lambda i: (i, gcol)),
                                   pl.BlockSpec((tm, D), lambda i: (i, gcol + 1)),
                                   row(D), full(wa), full(wb), full(wo), full(g2), full(b2)]
    args = [o1, o2, o3, l1, l2, l3, ob, proj, proj, x, wa, wb, wo, g2, b2]
    out_specs = [row(D), row(D)]
    out_shape = [jax.ShapeDtypeStruct((T, D), F32), jax.ShapeDtypeStruct((T, D), BF16)]
    if wr is not None:
        in_specs.append(full(wr))
        args.append(wr)
        out_specs.append(row(LANES))
        out_shape.append(jax.ShapeDtypeStruct((T, LANES), F32))
    return pl.pallas_call(
        functools.partial(_post_kernel, alpha=alpha, with_router=wr is not None),
        grid=(T // tm,),
        in_specs=in_specs,
        out_specs=out_specs,
        out_shape=out_shape,
        compiler_params=_cparams(("parallel",)),
    )(*args)


def _ffn_kernel(xb_ref, x_ref, wg_ref, wu_ref, wd_ref, g_ref, b_ref, xf_ref, xo_ref, acc_ref, *, alpha):
    f = pl.program_id(1)

    @pl.when(f == 0)
    def _():
        acc_ref[...] = jnp.zeros_like(acc_ref)

    xb = xb_ref[...]
    h = jax.nn.silu(_dot(xb, wg_ref[...])) * _dot(xb, wu_ref[...])
    acc_ref[...] += _dot(h.astype(BF16), wd_ref[...])

    @pl.when(f == pl.num_programs(1) - 1)
    def _():
        y = _layer_norm(alpha * x_ref[...] + acc_ref[...], g_ref[...], b_ref[...])
        xf_ref[...] = y
        xo_ref[...] = y.astype(BF16)


def _ffn(xb, x, wg, wu, wd, g, b, alpha, tm, tf):
    T, D = x.shape
    F = wg.shape[1]
    row = pl.BlockSpec((tm, D), lambda i, f: (i, 0))
    vec = pl.BlockSpec((1, D), lambda i, f: (0, 0))
    return pl.pallas_call(
        functools.partial(_ffn_kernel, alpha=alpha),
        grid=(T // tm, F // tf),
        in_specs=[row, row,
                  pl.BlockSpec((D, tf), lambda i, f: (0, f)),
                  pl.BlockSpec((D, tf), lambda i, f: (0, f)),
                  pl.BlockSpec((tf, D), lambda i, f: (f, 0)),
                  vec, vec],
        out_specs=[row, row],
        out_shape=[jax.ShapeDtypeStruct((T, D), F32), jax.ShapeDtypeStruct((T, D), BF16)],
        scratch_shapes=[pltpu.VMEM((tm, D), F32)],
        compiler_params=_cparams(("parallel", "arbitrary")),
    )(xb, x, wg, wu, wd, g.reshape(1, D), b.reshape(1, D))


def _route(logits, tm):
    T = logits.shape[0]
    top_logits, top_idx = lax.top_k(logits[:, :N_EXPERTS], TOP_K)
    gates = jax.nn.softmax(top_logits, axis=-1)
    n_assign = T * TOP_K
    flat_e = top_idx.reshape(n_assign).astype(jnp.int32)
    onehot = (flat_e[:, None] == jnp.arange(N_EXPERTS, dtype=jnp.int32)[None, :]).astype(jnp.int32)
    csum = jnp.cumsum(onehot, axis=0)
    counts = csum[-1]
    rank = jnp.sum((csum - onehot) * onehot, axis=1)
    padded = (counts + tm - 1) // tm * tm
    pad_end = jnp.cumsum(padded)
    pad_start = pad_end - padded
    dest = pad_start[flat_e] + rank
    n_blocks = -(-(n_assign + N_EXPERTS * (tm - 1)) // tm)
    n_rows = n_blocks * tm
    assign = jnp.arange(n_assign, dtype=jnp.int32)
    code = (assign // TOP_K) * 4 + (assign % TOP_K) * 2 + 1
    row_code = jnp.zeros((n_rows,), jnp.int32).at[dest].set(code)
    row_gate = jnp.zeros((n_rows,), F32).at[dest].set(gates.reshape(n_assign))
    block_start = jnp.arange(n_blocks, dtype=jnp.int32) * tm
    block_expert = jnp.minimum(jnp.searchsorted(pad_end, block_start, side='right'), N_EXPERTS - 1).astype(jnp.int32)
    n_valid = (pad_end[-1] // tm).astype(jnp.int32).reshape(1)
    return block_expert, n_valid, row_code, jnp.broadcast_to(row_gate[:, None], (n_rows, LANES))


def _moe_kernel(be_ref, nv_ref, code_ref, x_hbm, gate_ref, wg_ref, wu_ref, wd_ref, y_hbm,
                xg_ref, xb_ref, acc_ref, sem_in, sem_out, *, tm, n_tok):
    i = pl.program_id(0)
    f = pl.program_id(1)
    live = i < nv_ref[0]
    base = i * tm

    def row_in(r):
        tok = code_ref[base + r] >> 2
        return pltpu.make_async_copy(x_hbm.at[pl.ds(tok, 1)], xg_ref.at[pl.ds(r, 1)], sem_in)

    def row_out(r):
        code = code_ref[base + r]
        dst = ((code >> 1) & 1) * n_tok + (code >> 2)
        return pltpu.make_async_copy(acc_ref.at[pl.ds(r, 1)], y_hbm.at[pl.ds(dst, 1)], sem_out)

    @pl.when(live & (f == 0))
    def _():
        def start(r, c):
            row_in(r).start()
            return c

        def wait(r, c):
            row_in(r).wait()
            return c

        lax.fori_loop(0, tm, start, 0)
        lax.fori_loop(0, tm, wait, 0)
        xb_ref[...] = xg_ref[...].astype(BF16)
        acc_ref[...] = jnp.zeros_like(acc_ref)

    @pl.when(live)
    def _():
        xb = xb_ref[...]
        h = jax.nn.silu(_dot(xb, wg_ref[0])) * _dot(xb, wu_ref[0])
        acc_ref[...] += _dot(h.astype(BF16), wd_ref[0])

    @pl.when(live & (f == pl.num_programs(1) - 1))
    def _():
        gate = gate_ref[...]
        for c in range(acc_ref.shape[1] // LANES):
            sl = slice(c * LANES, (c + 1) * LANES)
            acc_ref[:, sl] = acc_ref[:, sl] * gate

        def start(r, c):
            @pl.when((code_ref[base + r] & 1) == 1)
            def _():
                row_out(r).start()
            return c

        def wait(r, c):
            @pl.when((code_ref[base + r] & 1) == 1)
            def _():
                row_out(r).wait()
            return c

        lax.fori_loop(0, tm, start, 0)
        lax.fori_loop(0, tm, wait, 0)


def _moe(x, route, wg, wu, wd, tm, tf):
    T, D = x.shape
    F = wg.shape[2]
    nf = F // tf
    block_expert, n_valid, row_code, row_gate = route
    n_blocks = block_expert.shape[0]

    def fidx(i, f, nv):
        return jnp.where(i < nv[0], f, nf - 1)

    grid_spec = pltpu.PrefetchScalarGridSpec(
        num_scalar_prefetch=3,
        grid=(n_blocks, nf),
        in_specs=[
            pl.BlockSpec(memory_space=pl.ANY),
            pl.BlockSpec((tm, LANES), lambda i, f, be, nv, rc: (i, 0)),
            pl.BlockSpec((1, D, tf), lambda i, f, be, nv, rc: (be[i], 0, fidx(i, f, nv))),
            pl.BlockSpec((1, D, tf), lambda i, f, be, nv, rc: (be[i], 0, fidx(i, f, nv))),
            pl.BlockSpec((1, tf, D), lambda i, f, be, nv, rc: (be[i], fidx(i, f, nv), 0)),
        ],
        out_specs=pl.BlockSpec(memory_space=pl.ANY),
        scratch_shapes=[pltpu.VMEM((tm, D), F32), pltpu.VMEM((tm, D), BF16), pltpu.VMEM((tm, D), F32),
                        pltpu.SemaphoreType.DMA(()), pltpu.SemaphoreType.DMA(())],
    )
    y = pl.pallas_call(
        functools.partial(_moe_kernel, tm=tm, n_tok=T),
        grid_spec=grid_spec,
        out_shape=jax.ShapeDtypeStruct((TOP_K * T, D), F32),
        compiler_params=_cparams(("arbitrary", "arbitrary")),
    )(block_expert, n_valid, row_code, x, row_gate, wg, wu, wd)
    return y


def _combine_kernel(x_ref, y0_ref, y1_ref, g_ref, b_ref, xf_ref, xb_ref, *, alpha):
    y = _layer_norm(alpha * x_ref[...] + (y0_ref[...] + y1_ref[...]), g_ref[...], b_ref[...])
    xf_ref[...] = y
    xb_ref[...] = y.astype(BF16)


def _combine(x, y, g, b, alpha, tm):
    T, D = x.shape
    n = T // tm
    row = pl.BlockSpec((tm, D), lambda i: (i, 0))
    vec = pl.BlockSpec((1, D), lambda i: (0, 0))
    return pl.pallas_call(
        functools.partial(_combine_kernel, alpha=alpha),
        grid=(n,),
        in_specs=[row, row, pl.BlockSpec((tm, D), lambda i: (n + i, 0)), vec, vec],
        out_specs=[row, row],
        out_shape=[jax.ShapeDtypeStruct((T, D), F32), jax.ShapeDtypeStruct((T, D), BF16)],
        compiler_params=_cparams(("parallel",)),
    )(x, y, y, g.reshape(1, D), b.reshape(1, D))


def _tiles(D):
    big = D >= 2048
    return dict(ln=512, proj=1024, attn_b=256, post=256 if big else 512, ffn_m=512, ffn_f=512,
                moe_m=512, moe_f=512, comb=512)


def kernel(x_prompt, x_sample, ln_in_g, ln_in_b, w_in, w_branch_a, w_branch_b, w_out, diff_lambda, diff_subln_g, ln_mix_g, ln_mix_b, ln_ffn_g, ln_ffn_b, ffn_w_gate, ffn_w_up, ffn_w_down, moe_router, moe_w_gate, moe_w_up, moe_w_down):
    depth = w_in.shape[0]
    D = x_prompt.shape[-1]
    alpha = (2 * depth) ** 0.25
    tl = _tiles(D)
    seqs = [(x_prompt.shape[0], x_prompt.shape[1]), (x_sample.shape[0], x_sample.shape[1])]
    x = jnp.concatenate([x_prompt.reshape(-1, D), x_sample.reshape(-1, D)], axis=0)
    T = x.shape[0]
    tab = _rope_tables(max(S for _, S in seqs))
    bf = lambda w: w.astype(BF16)

    xf, xb = _ln_in(x, ln_in_g, ln_in_b, tl['ln'])
    for l in range(depth):
        lam_init = 0.8 - 0.6 * math.exp(-0.3 * l)
        consts = jnp.full((1, LANES), lam_init, F32)
        proj = _proj(xb, bf(w_in[l]), tab, seqs, tl['proj'])

        oa = []
        for g in range(len(A_GROUPS)):
            prev, row0 = None, 0
            for n_seq, S in seqs:
                prev = _attn_a_group(proj, prev, g, n_seq, S, row0, T)
                row0 += n_seq * S
            oa.append(prev)
        ob, row0 = None, 0
        for n_seq, S in seqs:
            ob = _attn_b(proj, ob, diff_lambda[l], diff_subln_g[l], consts, n_seq, S, row0, T, tl['attn_b'])
            row0 += n_seq * S

        moe_layer = l % 2 == 1
        i = l // 2
        wr = None
        if moe_layer:
            wr = jnp.zeros((D, LANES), F32).at[:, :N_EXPERTS].set(moe_router[i])
        res = _post(oa, ob, proj, xf, bf(w_branch_a[l]), bf(w_branch_b[l]), bf(w_out[l]),
                    ln_mix_g[l], ln_mix_b[l], wr, alpha, tl['post'])
        if moe_layer:
            xf, xb, logits = res
            route = _route(logits, tl['moe_m'])
            y = _moe(xf, route, bf(moe_w_gate[i]), bf(moe_w_up[i]), bf(moe_w_down[i]), tl['moe_m'], tl['moe_f'])
            xf, xb = _combine(xf, y, ln_ffn_g[l], ln_ffn_b[l], alpha, tl['comb'])
        else:
            xf, xb = res
            xf, xb = _ffn(xb, xf, bf(ffn_w_gate[i]), bf(ffn_w_up[i]), bf(ffn_w_down[i]),
                          ln_ffn_g[l], ln_ffn_b[l], alpha, tl['ffn_m'], tl['ffn_f'])

    n_p = seqs[0][0] * seqs[0][1]
    return (xf[:n_p].reshape(x_prompt.shape), xf[n_p:].reshape(x_sample.shape))
```

```python
import functools
import math

import jax
import jax.numpy as jnp
from jax import lax
from jax.experimental import pallas as pl
from jax.experimental.pallas import tpu as pltpu

F32 = jnp.float32
BF16 = jnp.bfloat16

HEAD_DIM = 128
A_GROUPS = ((128, 1), (512, 4), (2048, 16))
A_HEADS_PER_GROUP = 4
A_OUT = A_HEADS_PER_GROUP * HEAD_DIM
A_QKV = len(A_GROUPS) * A_OUT
B_HEADS = 4
B_QK_DIM = 64
B_V_DIM = 2 * B_QK_DIM
B_COLS = B_HEADS * B_V_DIM
ROPE_THETA = 500000.0
ROPE_FRACTION = 4
N_EXPERTS = 8
TOP_K = 2
LN_EPS = 1e-5
MASK_VALUE = -1e30
GATE_COL0 = 3 * A_QKV + 3 * B_COLS

LANES = 128
COL_BLOCK = 512
VMEM_LIMIT = 56 * 1024 * 1024


def _cparams(sem):
    return pltpu.CompilerParams(dimension_semantics=sem, vmem_limit_bytes=VMEM_LIMIT)


def _layer_norm(y, g, b):
    mu = jnp.mean(y, axis=-1, keepdims=True)
    yc = y - mu
    var = jnp.mean(yc * yc, axis=-1, keepdims=True)
    return yc * lax.rsqrt(var + LN_EPS) * g + b


def _dot(a, b):
    return jnp.dot(a, b, preferred_element_type=F32)


def _dot_nt(a, b):
    return lax.dot_general(a, b, (((1,), (1,)), ((), ())), preferred_element_type=F32)


def _ln_in_kernel(x_ref, g_ref, b_ref, xf_ref, xb_ref):
    y = _layer_norm(x_ref[...], g_ref[...], b_ref[...])
    xf_ref[...] = y
    xb_ref[...] = y.astype(BF16)


def _ln_in(x, g, b, tm):
    T, D = x.shape
    row = pl.BlockSpec((tm, D), lambda i: (i, 0))
    vec = pl.BlockSpec((1, D), lambda i: (0, 0))
    return pl.pallas_call(
        _ln_in_kernel,
        grid=(T // tm,),
        in_specs=[row, vec, vec],
        out_specs=[row, row],
        out_shape=[jax.ShapeDtypeStruct((T, D), F32), jax.ShapeDtypeStruct((T, D), BF16)],
        compiler_params=_cparams(("parallel",)),
    )(x, g.reshape(1, D), b.reshape(1, D))


def _rope_tables(s_max):
    pos = jnp.arange(s_max, dtype=F32)[:, None]

    def tables(head, n_rep):
        rot = head // ROPE_FRACTION
        half = rot // 2
        inv_freq = ROPE_THETA ** (-jnp.arange(half, dtype=F32) / half)
        ang = pos * inv_freq[None, :]
        cos, sin = jnp.cos(ang), jnp.sin(ang)
        zero = jnp.zeros((s_max, head - rot), F32)
        zh = jnp.zeros((s_max, half), F32)
        c = jnp.concatenate([cos, cos, jnp.ones((s_max, head - rot), F32)], axis=1)
        s_up = jnp.concatenate([-sin, zh, zero], axis=1)
        s_dn = jnp.concatenate([zh, sin, zero], axis=1)
        return [jnp.tile(t, (1, n_rep)) for t in (c, s_up, s_dn)]

    return jnp.stack(tables(HEAD_DIM, 1) + tables(B_QK_DIM, 2), axis=0)


def _proj_kernel(x_ref, w_ref, tab_ref, o_ref):
    j = pl.program_id(1)
    acc = _dot(x_ref[...], w_ref[...])
    n_a = A_QKV // COL_BLOCK
    is_a = j < 2 * n_a
    jb = j - 3 * n_a
    is_b = (jb == 0) | (jb == 1)

    def rope(t0, half, scale):
        for c in range(COL_BLOCK // LANES):
            sl = slice(c * LANES, (c + 1) * LANES)
            xc = acc[:, sl]
            up = pltpu.roll(xc, LANES - half, 1)
            dn = pltpu.roll(xc, half, 1)
            y = xc * tab_ref[t0] + up * tab_ref[t0 + 1] + dn * tab_ref[t0 + 2]
            o_ref[:, sl] = (y * scale).astype(BF16)

    @pl.when(is_a)
    def _():
        rope(0, HEAD_DIM // ROPE_FRACTION // 2, jnp.where(j < n_a, HEAD_DIM ** -0.5, 1.0).astype(F32))

    @pl.when(is_b)
    def _():
        rope(3, B_QK_DIM // ROPE_FRACTION // 2, jnp.where(jb == 0, B_QK_DIM ** -0.5, 1.0).astype(F32))

    @pl.when(jnp.logical_not(is_a | is_b))
    def _():
        o_ref[...] = acc.astype(BF16)


def _proj(xb, w, tab, seqs, tm):
    T, D = xb.shape
    C = w.shape[1]
    bounds = []
    t0 = 0
    for n_seq, S in seqs:
        assert S % tm == 0
        bounds.append((t0 // tm, S // tm))
        t0 += n_seq * S
    assert t0 == T

    def pos_block(i):
        blk = (i - bounds[0][0]) % bounds[0][1]
        for first, per in bounds[1:]:
            blk = jnp.where(i >= first, (i - first) % per, blk)
        return blk

    return pl.pallas_call(
        _proj_kernel,
        grid=(T // tm, C // COL_BLOCK),
        in_specs=[
            pl.BlockSpec((tm, D), lambda i, j: (i, 0)),
            pl.BlockSpec((D, COL_BLOCK), lambda i, j: (0, j)),
            pl.BlockSpec((6, tm, LANES), lambda i, j: (0, pos_block(i), 0)),
        ],
        out_specs=pl.BlockSpec((tm, COL_BLOCK), lambda i, j: (i, j)),
        out_shape=jax.ShapeDtypeStruct((T, C), BF16),
        compiler_params=_cparams(("parallel", "arbitrary")),
    )(xb, w, tab)


def _attn_a_kernel(q_ref, k_ref, v_ref, *rest, L, tq, kw, n_side):
    o_ref, lse_ref = rest[-2], rest[-1]
    n_q = L // tq

    def body(qi, carry):
        q0 = pl.multiple_of(qi * tq, tq)
        start = pl.multiple_of(jnp.clip(q0 - n_side, 0, L - kw), n_side)
        qpos = q0 + lax.broadcasted_iota(jnp.int32, (tq, 1), 0)
        kpos = start + lax.broadcasted_iota(jnp.int32, (1, kw), 1)
        mask = jnp.abs(kpos - qpos) <= n_side
        for h in range(A_HEADS_PER_GROUP):
            sl = slice(h * HEAD_DIM, (h + 1) * HEAD_DIM)
            q = q_ref[pl.ds(q0, tq), sl]
            k = k_ref[pl.ds(start, kw), sl]
            v = v_ref[pl.ds(start, kw), sl]
            s = jnp.where(mask, _dot_nt(q, k), MASK_VALUE)
            m = jnp.max(s, axis=-1, keepdims=True)
            p = jnp.exp(s - m)
            den = jnp.sum(p, axis=-1, keepdims=True)
            o = _dot(p.astype(BF16), v) / den
            o_ref[pl.ds(q0, tq), sl] = o.astype(BF16)
            lse_ref[pl.ds(q0, tq), sl] = jnp.broadcast_to(m + jnp.log(den), (tq, HEAD_DIM))
        return carry

    lax.fori_loop(0, n_q, body, 0)


def _attn_a_dil_kernel(q_ref, k_ref, v_ref, *rest, S, d, tq, kw, n_side, unroll):
    o_ref, lse_ref, qf_ref, kf_ref, vf_ref = rest[-5:]
    L = S // d
    qf_ref[...] = q_ref[...].astype(F32)
    kf_ref[...] = k_ref[...].astype(F32)
    vf_ref[...] = v_ref[...].astype(F32)

    def body(r, carry):
        for qi in range(L // tq):
            q0 = qi * tq
            start = min(max(q0 - n_side, 0), L - kw)
            qpos = q0 + lax.broadcasted_iota(jnp.int32, (tq, 1), 0)
            kpos = start + lax.broadcasted_iota(jnp.int32, (1, kw), 1)
            mask = jnp.abs(kpos - qpos) <= n_side
            q = qf_ref[pl.ds(r + q0 * d, tq, stride=d), :].astype(BF16)
            k = kf_ref[pl.ds(r + start * d, kw, stride=d), :].astype(BF16)
            v = vf_ref[pl.ds(r + start * d, kw, stride=d), :].astype(BF16)
            s = jnp.where(mask, _dot_nt(q, k), MASK_VALUE)
            m = jnp.max(s, axis=-1, keepdims=True)
            p = jnp.exp(s - m)
            den = jnp.sum(p, axis=-1, keepdims=True)
            o_ref[pl.ds(r + q0 * d, tq, stride=d), :] = _dot(p.astype(BF16), v) / den
            lse_ref[pl.ds(r + q0 * d, tq, stride=d), :] = jnp.broadcast_to(m + jnp.log(den), (tq, HEAD_DIM))
        return carry

    lax.fori_loop(0, d, body, 0, unroll=unroll)


def _attn_a_group(proj, prev, g, n_seq, S, row0, T):
    window, d = A_GROUPS[g]
    n_side = window // (2 * d)
    L = S // d
    tq = min(128, L)
    kw = min(L, tq + 2 * n_side)
    assert L % tq == 0 and tq % n_side == 0 and (L - kw) % n_side == 0
    n_g = len(A_GROUPS)
    blk0 = row0 // S
    o_dtype = BF16 if d == 1 else F32
    if prev is None:
        prev = (jnp.zeros((T, A_OUT), o_dtype), jnp.zeros((T, A_OUT), F32))
    hbm = pl.BlockSpec(memory_space=pl.ANY)
    out_shape = [jax.ShapeDtypeStruct((T, A_OUT), o_dtype), jax.ShapeDtypeStruct((T, A_OUT), F32)]
    if d == 1:
        spec = lambda part: pl.BlockSpec((S, COL_BLOCK), lambda b: (blk0 + b, part * n_g + g))
        out_spec = pl.BlockSpec((S, A_OUT), lambda b: (blk0 + b, 0))
        return pl.pallas_call(
            functools.partial(_attn_a_kernel, L=L, tq=tq, kw=kw, n_side=n_side),
            grid=(n_seq,),
            in_specs=[spec(0), spec(1), spec(2), hbm, hbm],
            out_specs=[out_spec, out_spec],
            out_shape=out_shape,
            input_output_aliases={3: 0, 4: 1},
            compiler_params=_cparams(("parallel",)),
        )(proj, proj, proj, *prev)
    heads = n_g * A_HEADS_PER_GROUP
    spec = lambda part: pl.BlockSpec((S, HEAD_DIM), lambda b, h: (blk0 + b, part * heads + g * A_HEADS_PER_GROUP + h))
    out_spec = pl.BlockSpec((S, HEAD_DIM), lambda b, h: (blk0 + b, h))
    unroll = max(1, min(d, 8 // (L // tq)))
    return pl.pallas_call(
        functools.partial(_attn_a_dil_kernel, S=S, d=d, tq=tq, kw=kw, n_side=n_side, unroll=unroll),
        grid=(n_seq, A_HEADS_PER_GROUP),
        in_specs=[spec(0), spec(1), spec(2), hbm, hbm],
        out_specs=[out_spec, out_spec],
        out_shape=out_shape,
        input_output_aliases={3: 0, 4: 1},
        scratch_shapes=[pltpu.VMEM((S, HEAD_DIM), F32)] * 3,
        compiler_params=_cparams(("parallel", "parallel")),
    )(proj, proj, proj, *prev)


def _attn_b_kernel(q_ref, k_ref, v_ref, lam_ref, g_ref, c_ref, *rest):
    o_ref = rest[-1]
    q = q_ref[...]
    k = k_ref[...]
    v = v_ref[...]
    lane = lax.broadcasted_iota(jnp.int32, (1, 2 * B_QK_DIM), 1)
    zero = jnp.zeros_like(q)
    lam_init = c_ref[0:1, 0:1]
    lv = lam_ref[...]
    lam = (jnp.exp(jnp.sum(lv[0:1] * lv[1:2], axis=-1, keepdims=True))
           - jnp.exp(jnp.sum(lv[2:3] * lv[3:4], axis=-1, keepdims=True)) + lam_init)

    def softmax_v(qz):
        s = _dot_nt(qz, k)
        m = jnp.max(s, axis=-1, keepdims=True)
        p = jnp.exp(s - m)
        den = jnp.sum(p, axis=-1, keepdims=True)
        return _dot(p.astype(BF16), v) / den

    o = softmax_v(jnp.where(lane < B_QK_DIM, q, zero)) - lam * softmax_v(jnp.where(lane >= B_QK_DIM, q, zero))
    o = o * lax.rsqrt(jnp.mean(o * o, axis=-1, keepdims=True) + LN_EPS) * g_ref[...] * (1.0 - lam_init)
    o_ref[...] = o.astype(BF16)


def _attn_b(proj, prev, lam_vecs, subln_g, consts, n_seq, S, row0, T, tq):
    C = proj.shape[1]
    blk0 = row0 // S
    qb0 = row0 // tq
    n_qt = S // tq
    col = lambda part: (3 * A_QKV + part * B_COLS) // LANES
    in_specs = [
        pl.BlockSpec((tq, LANES), lambda b, h, i: (qb0 + b * n_qt + i, col(0) + h)),
        pl.BlockSpec((S, LANES), lambda b, h, i: (blk0 + b, col(1) + h)),
        pl.BlockSpec((S, LANES), lambda b, h, i: (blk0 + b, col(2) + h)),
        pl.BlockSpec((4, B_QK_DIM), lambda b, h, i: (0, 0)),
        pl.BlockSpec((1, B_V_DIM), lambda b, h, i: (0, 0)),
        pl.BlockSpec((1, LANES), lambda b, h, i: (0, 0)),
    ]
    if prev is None:
        prev = jnp.zeros((T, B_COLS), BF16)
    in_specs.append(pl.BlockSpec(memory_space=pl.ANY))
    args = [proj, proj, proj, lam_vecs, subln_g.reshape(1, B_V_DIM), consts, prev]
    aliases = {6: 0}
    return pl.pallas_call(
        _attn_b_kernel,
        grid=(n_seq, B_HEADS, n_qt),
        in_specs=in_specs,
        out_specs=pl.BlockSpec((tq, LANES), lambda b, h, i: (qb0 + b * n_qt + i, h)),
        out_shape=jax.ShapeDtypeStruct((T, B_COLS), BF16),
        input_output_aliases=aliases,
        compiler_params=_cparams(("parallel", "parallel", "arbitrary")),
    )(*args)


def _post_kernel(o1_ref, o2_ref, o3_ref, l1_ref, l2_ref, l3_ref, ob_ref, ga_ref, gb_ref, x_ref,
                 wa_ref, wb_ref, wo_ref, g_ref, b_ref, *rest, alpha, with_router):
    if with_router:
        wrh_ref, wrl_ref, xf_ref, xb_ref, lg_ref = rest
    else:
        xf_ref, xb_ref = rest
    l1, l2, l3 = l1_ref[...], l2_ref[...], l3_ref[...]
    m = jnp.maximum(jnp.maximum(l1, l2), l3)
    e1, e2, e3 = jnp.exp(l1 - m), jnp.exp(l2 - m), jnp.exp(l3 - m)
    oa = (e1 * o1_ref[...].astype(F32) + e2 * o2_ref[...].astype(F32) + e3 * o3_ref[...].astype(F32)) / (e1 + e2 + e3)
    ya = _dot(oa.astype(BF16), wa_ref[...])
    yb = _dot(ob_ref[...], wb_ref[...])
    mix = jax.nn.sigmoid(ga_ref[...].astype(F32)) * ya + jax.nn.sigmoid(gb_ref[...].astype(F32)) * yb
    z = _dot(mix.astype(BF16), wo_ref[...])
    y = _layer_norm(alpha * x_ref[...] + z, g_ref[...], b_ref[...])
    yh = y.astype(BF16)
    xf_ref[...] = y
    xb_ref[...] = yh
    if with_router:
        yl = (y - yh.astype(F32)).astype(BF16)
        lg_ref[...] = _dot(yh, wrh_ref[...]) + (_dot(yl, wrh_ref[...]) + _dot(yh, wrl_ref[...]))


def _post(oa, ob, proj, x, wa, wb, wo, g, b, wr, alpha, tm):
    T, D = x.shape
    assert GATE_COL0 % D == 0
    gcol = GATE_COL0 // D
    row = lambda w: pl.BlockSpec((tm, w), lambda i: (i, 0))
    full = lambda a: pl.BlockSpec(a.shape, lambda i: (0,) * a.ndim)
    g2, b2 = g.reshape(1, D), b.reshape(1, D)
    (o1, l1), (o2, l2), (o3, l3) = oa
    in_specs = [row(A_OUT)] * 6 + [row(B_COLS),
                                   pl.BlockSpec((tm, D), lambda i: (i, gcol)),
                                   pl.BlockSpec((tm, D), lambda i: (i, gcol + 1)),
                                   row(D), full(wa), full(wb), full(wo), full(g2), full(b2)]
    args = [o1, o2, o3, l1, l2, l3, ob, proj, proj, x, wa, wb, wo, g2, b2]
    out_specs = [row(D), row(D)]
    out_shape = [jax.ShapeDtypeStruct((T, D), F32), jax.ShapeDtypeStruct((T, D), BF16)]
    if wr is not None:
        wr_hi = wr.astype(BF16)
        wr_lo = (wr - wr_hi.astype(F32)).astype(BF16)
        in_specs += [full(wr_hi), full(wr_lo)]
        args += [wr_hi, wr_lo]
        out_specs.append(row(LANES))
        out_shape.append(jax.ShapeDtypeStruct((T, LANES), F32))
    return pl.pallas_call(
        functools.partial(_post_kernel, alpha=alpha, with_router=wr is not None),
        grid=(T // tm,),
        in_specs=in_specs,
        out_specs=out_specs,
        out_shape=out_shape,
        compiler_params=_cparams(("parallel",)),
    )(*args)


def _ffn_kernel(xb_ref, x_ref, wg_ref, wu_ref, wd_ref, g_ref, b_ref, xf_ref, xo_ref, acc_ref, *, alpha):
    f = pl.program_id(1)

    @pl.when(f == 0)
    def _():
        acc_ref[...] = jnp.zeros_like(acc_ref)

    xb = xb_ref[...]
    h = jax.nn.silu(_dot(xb, wg_ref[...])) * _dot(xb, wu_ref[...])
    acc_ref[...] += _dot(h.astype(BF16), wd_ref[...])

    @pl.when(f == pl.num_programs(1) - 1)
    def _():
        y = _layer_norm(alpha * x_ref[...] + acc_ref[...], g_ref[...], b_ref[...])
        xf_ref[...] = y
        xo_ref[...] = y.astype(BF16)


def _ffn(xb, x, wg, wu, wd, g, b, alpha, tm, tf):
    T, D = x.shape
    F = wg.shape[1]
    row = pl.BlockSpec((tm, D), lambda i, f: (i, 0))
    vec = pl.BlockSpec((1, D), lambda i, f: (0, 0))
    return pl.pallas_call(
        functools.partial(_ffn_kernel, alpha=alpha),
        grid=(T // tm, F // tf),
        in_specs=[row, row,
                  pl.BlockSpec((D, tf), lambda i, f: (0, f)),
                  pl.BlockSpec((D, tf), lambda i, f: (0, f)),
                  pl.BlockSpec((tf, D), lambda i, f: (f, 0)),
                  vec, vec],
        out_specs=[row, row],
        out_shape=[jax.ShapeDtypeStruct((T, D), F32), jax.ShapeDtypeStruct((T, D), BF16)],
        scratch_shapes=[pltpu.VMEM((tm, D), F32)],
        compiler_params=_cparams(("parallel", "arbitrary")),
    )(xb, x, wg, wu, wd, g.reshape(1, D), b.reshape(1, D))


def _route(logits, tm):
    T = logits.shape[0]
    top_logits, top_idx = lax.top_k(logits[:, :N_EXPERTS], TOP_K)
    gates = jax.nn.softmax(top_logits, axis=-1)
    n_assign = T * TOP_K
    flat_e = top_idx.reshape(n_assign).astype(jnp.int32)
    onehot = (flat_e[:, None] == jnp.arange(N_EXPERTS, dtype=jnp.int32)[None, :]).astype(jnp.int32)
    csum = jnp.cumsum(onehot, axis=0)
    counts = csum[-1]
    rank = jnp.sum((csum - onehot) * onehot, axis=1)
    padded = (counts + tm - 1) // tm * tm
    pad_end = jnp.cumsum(padded)
    pad_start = pad_end - padded
    dest = pad_start[flat_e] + rank
    n_blocks = -(-(n_assign + N_EXPERTS * (tm - 1)) // tm)
    n_rows = n_blocks * tm
    row_assign = jnp.full((n_rows,), -1, jnp.int32).at[dest].set(jnp.arange(n_assign, dtype=jnp.int32))
    used = row_assign >= 0
    row_code = jnp.where(used, (row_assign // TOP_K) * 4 + (row_assign % TOP_K) * 2 + 1, 0)
    row_gate = jnp.where(used, gates.reshape(n_assign)[jnp.maximum(row_assign, 0)], 0.0)
    block_start = jnp.arange(n_blocks, dtype=jnp.int32) * tm
    block_expert = jnp.minimum(jnp.searchsorted(pad_end, block_start, side='right'), N_EXPERTS - 1).astype(jnp.int32)
    n_valid = (pad_end[-1] // tm).astype(jnp.int32).reshape(1)
    return block_expert, n_valid, row_code, jnp.broadcast_to(row_gate[:, None], (n_rows, LANES))


def _moe_kernel(be_ref, nv_ref, code_ref, x_hbm, gate_ref, wg_ref, wu_ref, wd_ref, y_hbm,
                xg_ref, xb_ref, acc_ref, yo_ref, sem_in, sem_out, *, tm, n_tok, rows_per_step):
    i = pl.program_id(0)
    f = pl.program_id(1)
    n_f = pl.num_programs(1)
    n_live = nv_ref[0]
    live = i < n_live
    slot = i % 2

    def gather_rows(blk, buf, lo, hi):
        def body(r, c):
            tok = code_ref[blk * tm + r] >> 2
            pltpu.make_async_copy(x_hbm.at[pl.ds(tok, 1)], xg_ref.at[buf, pl.ds(r, 1)], sem_in.at[buf]).start()
            return c

        lax.fori_loop(lo, hi, body, 0)

    def wait_scatter():
        pltpu.make_async_copy(yo_ref, yo_ref, sem_out).wait()

    @pl.when((i == 0) & (f == 0))
    def _():
        gather_rows(0, 0, 0, tm)
        yo_ref[...] = jnp.zeros_like(yo_ref)
        spare = pltpu.make_async_copy(yo_ref, y_hbm.at[pl.ds(TOP_K * n_tok, tm)], sem_out)
        spare.start()
        spare.wait()

    @pl.when(live & (f == 0))
    def _():
        pltpu.make_async_copy(xg_ref.at[slot], xg_ref.at[slot], sem_in.at[slot]).wait()
        xb_ref[...] = xg_ref[slot].astype(BF16)
        acc_ref[...] = jnp.zeros_like(acc_ref)

    @pl.when(live & (i + 1 < n_live))
    def _():
        gather_rows(i + 1, 1 - slot, f * rows_per_step, jnp.minimum((f + 1) * rows_per_step, tm))

    @pl.when(live)
    def _():
        xb = xb_ref[...]
        h = jax.nn.silu(_dot(xb, wg_ref[0])) * _dot(xb, wu_ref[0])
        acc_ref[...] += _dot(h.astype(BF16), wd_ref[0])

    @pl.when(live & (f == n_f - 1))
    def _():
        @pl.when(i > 0)
        def _():
            wait_scatter()

        gate = gate_ref[...]
        for c in range(acc_ref.shape[1] // LANES):
            sl = slice(c * LANES, (c + 1) * LANES)
            yo_ref[:, sl] = acc_ref[:, sl] * gate

        def body(r, c):
            code = code_ref[i * tm + r]
            dst = jnp.where((code & 1) == 1, ((code >> 1) & 1) * n_tok + (code >> 2), TOP_K * n_tok + r)
            pltpu.make_async_copy(yo_ref.at[pl.ds(r, 1)], y_hbm.at[pl.ds(dst, 1)], sem_out).start()
            return c

        lax.fori_loop(0, tm, body, 0)

    @pl.when((i == pl.num_programs(0) - 1) & (f == n_f - 1))
    def _():
        wait_scatter()


def _moe(x, route, wg, wu, wd, tm, tf):
    T, D = x.shape
    F = wg.shape[2]
    nf = F // tf
    block_expert, n_valid, row_code, row_gate = route
    n_blocks = block_expert.shape[0]

    def fidx(i, f, nv):
        return jnp.where(i < nv[0], f, nf - 1)

    grid_spec = pltpu.PrefetchScalarGridSpec(
        num_scalar_prefetch=3,
        grid=(n_blocks, nf),
        in_specs=[
            pl.BlockSpec(memory_space=pl.ANY),
            pl.BlockSpec((tm, LANES), lambda i, f, be, nv, rc: (i, 0)),
            pl.BlockSpec((1, D, tf), lambda i, f, be, nv, rc: (be[i], 0, fidx(i, f, nv))),
            pl.BlockSpec((1, D, tf), lambda i, f, be, nv, rc: (be[i], 0, fidx(i, f, nv))),
            pl.BlockSpec((1, tf, D), lambda i, f, be, nv, rc: (be[i], fidx(i, f, nv), 0)),
        ],
        out_specs=pl.BlockSpec(memory_space=pl.ANY),
        scratch_shapes=[pltpu.VMEM((2, tm, D), F32), pltpu.VMEM((tm, D), BF16), pltpu.VMEM((tm, D), F32),
                        pltpu.VMEM((tm, D), F32), pltpu.SemaphoreType.DMA((2,)), pltpu.SemaphoreType.DMA(())],
    )
    y = pl.pallas_call(
        functools.partial(_moe_kernel, tm=tm, n_tok=T, rows_per_step=-(-tm // nf)),
        grid_spec=grid_spec,
        out_shape=jax.ShapeDtypeStruct((TOP_K * T + tm, D), F32),
        compiler_params=_cparams(("arbitrary", "arbitrary")),
    )(block_expert, n_valid, row_code, x, row_gate, wg, wu, wd)
    return y


def _combine_kernel(x_ref, y0_ref, y1_ref, g_ref, b_ref, xf_ref, *rest, alpha):
    y = _layer_norm(alpha * x_ref[...] + (y0_ref[...] + y1_ref[...]), g_ref[...], b_ref[...])
    xf_ref[...] = y
    for xb_ref in rest:
        xb_ref[...] = y.astype(BF16)


def _combine(x, y, g, b, alpha, tm, row0, n_rows, with_bf16):
    T, D = x.shape
    n = T // tm
    i0 = row0 // tm
    row = pl.BlockSpec((tm, D), lambda i: (i, 0))
    vec = pl.BlockSpec((1, D), lambda i: (0, 0))
    out_specs = [row, row] if with_bf16 else [row]
    out_shape = [jax.ShapeDtypeStruct((n_rows, D), F32)]
    if with_bf16:
        out_shape.append(jax.ShapeDtypeStruct((n_rows, D), BF16))
    return pl.pallas_call(
        functools.partial(_combine_kernel, alpha=alpha),
        grid=(n_rows // tm,),
        in_specs=[pl.BlockSpec((tm, D), lambda i: (i0 + i, 0)),
                  pl.BlockSpec((tm, D), lambda i: (i0 + i, 0)),
                  pl.BlockSpec((tm, D), lambda i: (n + i0 + i, 0)), vec, vec],
        out_specs=out_specs,
        out_shape=out_shape,
        compiler_params=_cparams(("parallel",)),
    )(x, y, y, g.reshape(1, D), b.reshape(1, D))


def _tiles(D):
    big = D >= 2048
    return dict(ln=512, proj=1024, attn_b=256, post=256 if big else 512, ffn_m=512, ffn_f=512,
                moe_m=512, moe_f=512, comb=512)


def kernel(x_prompt, x_sample, ln_in_g, ln_in_b, w_in, w_branch_a, w_branch_b, w_out, diff_lambda, diff_subln_g, ln_mix_g, ln_mix_b, ln_ffn_g, ln_ffn_b, ffn_w_gate, ffn_w_up, ffn_w_down, moe_router, moe_w_gate, moe_w_up, moe_w_down):
    depth = w_in.shape[0]
    D = x_prompt.shape[-1]
    alpha = (2 * depth) ** 0.25
    tl = _tiles(D)
    seqs = [(x_prompt.shape[0], x_prompt.shape[1]), (x_sample.shape[0], x_sample.shape[1])]
    x = jnp.concatenate([x_prompt.reshape(-1, D), x_sample.reshape(-1, D)], axis=0)
    T = x.shape[0]
    tab = _rope_tables(max(S for _, S in seqs))
    bf = lambda w: w.astype(BF16)

    xf, xb = _ln_in(x, ln_in_g, ln_in_b, tl['ln'])
    for l in range(depth):
        lam_init = 0.8 - 0.6 * math.exp(-0.3 * l)
        consts = jnp.full((1, LANES), lam_init, F32)
        proj = _proj(xb, bf(w_in[l]), tab, seqs, tl['proj'])

        oa = []
        for g in range(len(A_GROUPS)):
            prev, row0 = None, 0
            for n_seq, S in seqs:
                prev = _attn_a_group(proj, prev, g, n_seq, S, row0, T)
                row0 += n_seq * S
            oa.append(prev)
        ob, row0 = None, 0
        for n_seq, S in seqs:
            ob = _attn_b(proj, ob, diff_lambda[l], diff_subln_g[l], consts, n_seq, S, row0, T, tl['attn_b'])
            row0 += n_seq * S

        moe_layer = l % 2 == 1
        i = l // 2
        wr = None
        if moe_layer:
            wr = jnp.zeros((D, LANES), F32).at[:, :N_EXPERTS].set(moe_router[i])
        res = _post(oa, ob, proj, xf, bf(w_branch_a[l]), bf(w_branch_b[l]), bf(w_out[l]),
                    ln_mix_g[l], ln_mix_b[l], wr, alpha, tl['post'])
        if moe_layer:
            xf, xb, logits = res
            route = _route(logits, tl['moe_m'])
            y = _moe(xf, route, bf(moe_w_gate[i]), bf(moe_w_up[i]), bf(moe_w_down[i]), tl['moe_m'], tl['moe_f'])
            if l == depth - 1:
                n_p = seqs[0][0] * seqs[0][1]
                outs = [_combine(xf, y, ln_ffn_g[l], ln_ffn_b[l], alpha, tl['comb'], r0, n, False)[0]
                        for r0, n in ((0, n_p), (n_p, T - n_p))]
                return (outs[0].reshape(x_prompt.shape), outs[1].reshape(x_sample.shape))
            xf, xb = _combine(xf, y, ln_ffn_g[l], ln_ffn_b[l], alpha, tl['comb'], 0, T, True)
        else:
            xf, xb = res
            xf, xb = _ffn(xb, xf, bf(ffn_w_gate[i]), bf(ffn_w_up[i]), bf(ffn_w_down[i]),
                          ln_ffn_g[l], ln_ffn_b[l], alpha, tl['ffn_m'], tl['ffn_f'])

    n_p = seqs[0][0] * seqs[0][1]
    return (xf[:n_p].reshape(x_prompt.shape), xf[n_p:].reshape(x_sample.shape))
```

```python
import functools
import math

import jax
import jax.numpy as jnp
from jax import lax
from jax.experimental import pallas as pl
from jax.experimental.pallas import tpu as pltpu

F32 = jnp.float32
BF16 = jnp.bfloat16

HEAD_DIM = 128
A_GROUPS = ((128, 1), (512, 4), (2048, 16))
A_HEADS_PER_GROUP = 4
A_OUT = A_HEADS_PER_GROUP * HEAD_DIM
A_QKV = len(A_GROUPS) * A_OUT
B_HEADS = 4
B_QK_DIM = 64
B_V_DIM = 2 * B_QK_DIM
B_COLS = B_HEADS * B_V_DIM
ROPE_THETA = 500000.0
ROPE_FRACTION = 4
N_EXPERTS = 8
TOP_K = 2
LN_EPS = 1e-5
MASK_VALUE = -1e30
GATE_COL0 = 3 * A_QKV + 3 * B_COLS

LANES = 128
COL_BLOCK = 512
VMEM_LIMIT = 56 * 1024 * 1024


def _cparams(sem):
    return pltpu.CompilerParams(dimension_semantics=sem, vmem_limit_bytes=VMEM_LIMIT)


def _layer_norm(y, g, b):
    mu = jnp.mean(y, axis=-1, keepdims=True)
    yc = y - mu
    var = jnp.mean(yc * yc, axis=-1, keepdims=True)
    return yc * lax.rsqrt(var + LN_EPS) * g + b


def _dot(a, b):
    return jnp.dot(a, b, preferred_element_type=F32)


def _dot_nt(a, b):
    return lax.dot_general(a, b, (((1,), (1,)), ((), ())), preferred_element_type=F32)


def _ln_in_kernel(x_ref, g_ref, b_ref, xf_ref, xb_ref):
    y = _layer_norm(x_ref[...], g_ref[...], b_ref[...])
    xf_ref[...] = y
    xb_ref[...] = y.astype(BF16)


def _ln_in(x, g, b, tm):
    T, D = x.shape
    row = pl.BlockSpec((tm, D), lambda i: (i, 0))
    vec = pl.BlockSpec((1, D), lambda i: (0, 0))
    return pl.pallas_call(
        _ln_in_kernel,
        grid=(T // tm,),
        in_specs=[row, vec, vec],
        out_specs=[row, row],
        out_shape=[jax.ShapeDtypeStruct((T, D), F32), jax.ShapeDtypeStruct((T, D), BF16)],
        compiler_params=_cparams(("parallel",)),
    )(x, g.reshape(1, D), b.reshape(1, D))


def _rope_tables(s_max):
    pos = jnp.arange(s_max, dtype=F32)[:, None]

    def tables(head, n_rep):
        rot = head // ROPE_FRACTION
        half = rot // 2
        inv_freq = ROPE_THETA ** (-jnp.arange(half, dtype=F32) / half)
        ang = pos * inv_freq[None, :]
        cos, sin = jnp.cos(ang), jnp.sin(ang)
        zero = jnp.zeros((s_max, head - rot), F32)
        zh = jnp.zeros((s_max, half), F32)
        c = jnp.concatenate([cos, cos, jnp.ones((s_max, head - rot), F32)], axis=1)
        s_up = jnp.concatenate([-sin, zh, zero], axis=1)
        s_dn = jnp.concatenate([zh, sin, zero], axis=1)
        return [jnp.tile(t, (1, n_rep)) for t in (c, s_up, s_dn)]

    return jnp.stack(tables(HEAD_DIM, 1) + tables(B_QK_DIM, 2), axis=0)


def _proj_kernel(x_ref, w_ref, tab_ref, o_ref):
    j = pl.program_id(1)
    n_a = A_QKV // COL_BLOCK
    is_a = j < 2 * n_a
    jb = j - 3 * n_a
    is_b = (jb == 0) | (jb == 1)

    def rope(t0, half, scale):
        acc = _dot(x_ref[...], w_ref[...])
        for c in range(COL_BLOCK // LANES):
            sl = slice(c * LANES, (c + 1) * LANES)
            xc = acc[:, sl]
            up = pltpu.roll(xc, LANES - half, 1)
            dn = pltpu.roll(xc, half, 1)
            y = xc * tab_ref[t0] + up * tab_ref[t0 + 1] + dn * tab_ref[t0 + 2]
            o_ref[:, sl] = (y * scale).astype(BF16)

    @pl.when(is_a)
    def _():
        rope(0, HEAD_DIM // ROPE_FRACTION // 2, jnp.where(j < n_a, HEAD_DIM ** -0.5, 1.0).astype(F32))

    @pl.when(is_b)
    def _():
        rope(3, B_QK_DIM // ROPE_FRACTION // 2, jnp.where(jb == 0, B_QK_DIM ** -0.5, 1.0).astype(F32))

    @pl.when(jnp.logical_not(is_a | is_b))
    def _():
        o_ref[...] = _dot(x_ref[...], w_ref[...]).astype(BF16)


def _proj(xb, w, l, tab, seqs, tm):
    T, D = xb.shape
    C = w.shape[2]
    bounds = []
    t0 = 0
    for n_seq, S in seqs:
        assert S % tm == 0
        bounds.append((t0 // tm, S // tm))
        t0 += n_seq * S
    assert t0 == T

    def pos_block(i):
        blk = (i - bounds[0][0]) % bounds[0][1]
        for first, per in bounds[1:]:
            blk = jnp.where(i >= first, (i - first) % per, blk)
        return blk

    return pl.pallas_call(
        _proj_kernel,
        grid=(T // tm, C // COL_BLOCK),
        in_specs=[
            pl.BlockSpec((tm, D), lambda i, j: (i, 0)),
            pl.BlockSpec((None, D, COL_BLOCK), lambda i, j: (l, 0, j)),
            pl.BlockSpec((6, tm, LANES), lambda i, j: (0, pos_block(i), 0)),
        ],
        out_specs=pl.BlockSpec((tm, COL_BLOCK), lambda i, j: (i, j)),
        out_shape=jax.ShapeDtypeStruct((T, C), BF16),
        compiler_params=_cparams(("parallel", "arbitrary")),
    )(xb, w, tab)


def _attn_a_kernel(q_ref, k_ref, v_ref, *rest, L, tq, kw, n_side):
    o_ref, lse_ref = rest[-2], rest[-1]
    n_q = L // tq

    def body(qi, carry):
        q0 = pl.multiple_of(qi * tq, tq)
        start = pl.multiple_of(jnp.clip(q0 - n_side, 0, L - kw), n_side)
        qpos = q0 + lax.broadcasted_iota(jnp.int32, (tq, 1), 0)
        kpos = start + lax.broadcasted_iota(jnp.int32, (1, kw), 1)
        mask = jnp.abs(kpos - qpos) <= n_side
        for h in range(A_HEADS_PER_GROUP):
            sl = slice(h * HEAD_DIM, (h + 1) * HEAD_DIM)
            q = q_ref[pl.ds(q0, tq), sl]
            k = k_ref[pl.ds(start, kw), sl]
            v = v_ref[pl.ds(start, kw), sl]
            s = jnp.where(mask, _dot_nt(q, k), MASK_VALUE)
            m = jnp.max(s, axis=-1, keepdims=True)
            p = jnp.exp(s - m)
            den = jnp.sum(p, axis=-1, keepdims=True)
            o = _dot(p.astype(BF16), v) / den
            o_ref[pl.ds(q0, tq), sl] = o.astype(BF16)
            lse_ref[pl.ds(q0, tq), sl] = jnp.broadcast_to(m + jnp.log(den), (tq, HEAD_DIM))
        return carry

    lax.fori_loop(0, n_q, body, 0)


def _attn_a_dil_kernel(q_ref, k_ref, v_ref, *rest, S, d, tq, kw, n_side, unroll):
    o_ref, lse_ref, qf_ref, kf_ref, vf_ref = rest[-5:]
    L = S // d
    qf_ref[...] = q_ref[...].astype(F32)
    kf_ref[...] = k_ref[...].astype(F32)
    vf_ref[...] = v_ref[...].astype(F32)

    def body(r, carry):
        for qi in range(L // tq):
            q0 = qi * tq
            start = min(max(q0 - n_side, 0), L - kw)
            qpos = q0 + lax.broadcasted_iota(jnp.int32, (tq, 1), 0)
            kpos = start + lax.broadcasted_iota(jnp.int32, (1, kw), 1)
            mask = jnp.abs(kpos - qpos) <= n_side
            q = qf_ref[pl.ds(r + q0 * d, tq, stride=d), :].astype(BF16)
            k = kf_ref[pl.ds(r + start * d, kw, stride=d), :].astype(BF16)
            v = vf_ref[pl.ds(r + start * d, kw, stride=d), :].astype(BF16)
            s = jnp.where(mask, _dot_nt(q, k), MASK_VALUE)
            m = jnp.max(s, axis=-1, keepdims=True)
            p = jnp.exp(s - m)
            den = jnp.sum(p, axis=-1, keepdims=True)
            o_ref[pl.ds(r + q0 * d, tq, stride=d), :] = _dot(p.astype(BF16), v) / den
            lse_ref[pl.ds(r + q0 * d, tq, stride=d), :] = jnp.broadcast_to(m + jnp.log(den), (tq, HEAD_DIM))
        return carry

    lax.fori_loop(0, d, body, 0, unroll=unroll)


def _attn_a_group(proj, prev, g, n_seq, S, row0, T):
    window, d = A_GROUPS[g]
    n_side = window // (2 * d)
    L = S // d
    tq = min(128, L)
    kw = min(L, tq + 2 * n_side)
    assert L % tq == 0 and tq % n_side == 0 and (L - kw) % n_side == 0
    n_g = len(A_GROUPS)
    blk0 = row0 // S
    o_dtype = BF16 if d == 1 else F32
    if prev is None:
        prev = (jnp.zeros((T, A_OUT), o_dtype), jnp.zeros((T, A_OUT), F32))
    hbm = pl.BlockSpec(memory_space=pl.ANY)
    out_shape = [jax.ShapeDtypeStruct((T, A_OUT), o_dtype), jax.ShapeDtypeStruct((T, A_OUT), F32)]
    if d == 1:
        spec = lambda part: pl.BlockSpec((S, COL_BLOCK), lambda b: (blk0 + b, part * n_g + g))
        out_spec = pl.BlockSpec((S, A_OUT), lambda b: (blk0 + b, 0))
        return pl.pallas_call(
            functools.partial(_attn_a_kernel, L=L, tq=tq, kw=kw, n_side=n_side),
            grid=(n_seq,),
            in_specs=[spec(0), spec(1), spec(2), hbm, hbm],
            out_specs=[out_spec, out_spec],
            out_shape=out_shape,
            input_output_aliases={3: 0, 4: 1},
            compiler_params=_cparams(("parallel",)),
        )(proj, proj, proj, *prev)
    heads = n_g * A_HEADS_PER_GROUP
    spec = lambda part: pl.BlockSpec((S, HEAD_DIM), lambda b, h: (blk0 + b, part * heads + g * A_HEADS_PER_GROUP + h))
    out_spec = pl.BlockSpec((S, HEAD_DIM), lambda b, h: (blk0 + b, h))
    unroll = max(1, min(d, 8 // (L // tq)))
    return pl.pallas_call(
        functools.partial(_attn_a_dil_kernel, S=S, d=d, tq=tq, kw=kw, n_side=n_side, unroll=unroll),
        grid=(n_seq, A_HEADS_PER_GROUP),
        in_specs=[spec(0), spec(1), spec(2), hbm, hbm],
        out_specs=[out_spec, out_spec],
        out_shape=out_shape,
        input_output_aliases={3: 0, 4: 1},
        scratch_shapes=[pltpu.VMEM((S, HEAD_DIM), F32)] * 3,
        compiler_params=_cparams(("parallel", "parallel")),
    )(proj, proj, proj, *prev)


def _attn_b_kernel(q_ref, k_ref, v_ref, lam_ref, g_ref, c_ref, *rest):
    o_ref = rest[-1]
    q = q_ref[...]
    k = k_ref[...]
    v = v_ref[...]
    lane = lax.broadcasted_iota(jnp.int32, (1, 2 * B_QK_DIM), 1)
    zero = jnp.zeros_like(q)
    lam_init = c_ref[0:1, 0:1]
    lv = lam_ref[...]
    lam = (jnp.exp(jnp.sum(lv[0:1] * lv[1:2], axis=-1, keepdims=True))
           - jnp.exp(jnp.sum(lv[2:3] * lv[3:4], axis=-1, keepdims=True)) + lam_init)

    def softmax(qz):
        s = _dot_nt(qz, k)
        m = jnp.max(s, axis=-1, keepdims=True)
        p = jnp.exp(s - m)
        return p, 1.0 / jnp.sum(p, axis=-1, keepdims=True)

    p1, r1 = softmax(jnp.where(lane < B_QK_DIM, q, zero))
    p2, r2 = softmax(jnp.where(lane >= B_QK_DIM, q, zero))
    o = _dot((p1 * r1 - p2 * (lam * r2)).astype(BF16), v)
    o = o * lax.rsqrt(jnp.mean(o * o, axis=-1, keepdims=True) + LN_EPS) * g_ref[...] * (1.0 - lam_init)
    o_ref[...] = o.astype(BF16)


def _attn_b(proj, prev, lam_vecs, subln_g, consts, n_seq, S, row0, T, tq):
    C = proj.shape[1]
    blk0 = row0 // S
    qb0 = row0 // tq
    n_qt = S // tq
    col = lambda part: (3 * A_QKV + part * B_COLS) // LANES
    in_specs = [
        pl.BlockSpec((tq, LANES), lambda b, h, i: (qb0 + b * n_qt + i, col(0) + h)),
        pl.BlockSpec((S, LANES), lambda b, h, i: (blk0 + b, col(1) + h)),
        pl.BlockSpec((S, LANES), lambda b, h, i: (blk0 + b, col(2) + h)),
        pl.BlockSpec((4, B_QK_DIM), lambda b, h, i: (0, 0)),
        pl.BlockSpec((1, B_V_DIM), lambda b, h, i: (0, 0)),
        pl.BlockSpec((1, LANES), lambda b, h, i: (0, 0)),
    ]
    if prev is None:
        prev = jnp.zeros((T, B_COLS), BF16)
    in_specs.append(pl.BlockSpec(memory_space=pl.ANY))
    args = [proj, proj, proj, lam_vecs, subln_g.reshape(1, B_V_DIM), consts, prev]
    aliases = {6: 0}
    return pl.pallas_call(
        _attn_b_kernel,
        grid=(n_seq, B_HEADS, n_qt),
        in_specs=in_specs,
        out_specs=pl.BlockSpec((tq, LANES), lambda b, h, i: (qb0 + b * n_qt + i, h)),
        out_shape=jax.ShapeDtypeStruct((T, B_COLS), BF16),
        input_output_aliases=aliases,
        compiler_params=_cparams(("parallel", "parallel", "arbitrary")),
    )(*args)


def _post_kernel(o1_ref, o2_ref, o3_ref, l1_ref, l2_ref, l3_ref, ob_ref, ga_ref, gb_ref, x_ref,
                 wa_ref, wb_ref, wo_ref, g_ref, b_ref, *rest, alpha, with_router):
    if with_router:
        wrh_ref, wrl_ref, xf_ref, xb_ref, lg_ref = rest
    else:
        xf_ref, xb_ref = rest
    l1, l2, l3 = l1_ref[...], l2_ref[...], l3_ref[...]
    m = jnp.maximum(jnp.maximum(l1, l2), l3)
    e1, e2, e3 = jnp.exp(l1 - m), jnp.exp(l2 - m), jnp.exp(l3 - m)
    oa = (e1 * o1_ref[...].astype(F32) + e2 * o2_ref[...].astype(F32) + e3 * o3_ref[...].astype(F32)) / (e1 + e2 + e3)
    ya = _dot(oa.astype(BF16), wa_ref[...])
    yb = _dot(ob_ref[...], wb_ref[...])
    mix = jax.nn.sigmoid(ga_ref[...].astype(F32)) * ya + jax.nn.sigmoid(gb_ref[...].astype(F32)) * yb
    z = _dot(mix.astype(BF16), wo_ref[...])
    y = _layer_norm(alpha * x_ref[...] + z, g_ref[...], b_ref[...])
    yh = y.astype(BF16)
    xf_ref[...] = y
    xb_ref[...] = yh
    if with_router:
        yl = (y - yh.astype(F32)).astype(BF16)
        lg_ref[...] = _dot(yh, wrh_ref[...]) + (_dot(yl, wrh_ref[...]) + _dot(yh, wrl_ref[...]))


def _post(oa, ob, proj, x, wa, wb, wo, l, g, b, wr, alpha, tm):
    T, D = x.shape
    assert GATE_COL0 % D == 0
    gcol = GATE_COL0 // D
    row = lambda w: pl.BlockSpec((tm, w), lambda i: (i, 0))
    full = lambda a: pl.BlockSpec(a.shape, lambda i: (0,) * a.ndim)
    layer = lambda a: pl.BlockSpec((None,) + a.shape[1:], lambda i: (l, 0, 0))
    g2, b2 = g.reshape(1, D), b.reshape(1, D)
    (o1, l1), (o2, l2), (o3, l3) = oa
    in_specs = [row(A_OUT)] * 6 + [row(B_COLS),
                                   pl.BlockSpec((tm, D), lambda i: (i, gcol)),
                                   pl.BlockSpec((tm, D), lambda i: (i, gcol + 1)),
                                   row(D), layer(wa), layer(wb), layer(wo), full(g2), full(b2)]
    args = [o1, o2, o3, l1, l2, l3, ob, proj, proj, x, wa, wb, wo, g2, b2]
    out_specs = [row(D), row(D)]
    out_shape = [jax.ShapeDtypeStruct((T, D), F32), jax.ShapeDtypeStruct((T, D), BF16)]
    if wr is not None:
        wr_hi = wr.astype(BF16)
        wr_lo = (wr - wr_hi.astype(F32)).astype(BF16)
        in_specs += [full(wr_hi), full(wr_lo)]
        args += [wr_hi, wr_lo]
        out_specs.append(row(LANES))
        out_shape.append(jax.ShapeDtypeStruct((T, LANES), F32))
    return pl.pallas_call(
        functools.partial(_post_kernel, alpha=alpha, with_router=wr is not None),
        grid=(T // tm,),
        in_specs=in_specs,
        out_specs=out_specs,
        out_shape=out_shape,
        compiler_params=_cparams(("parallel",)),
    )(*args)


def _ffn_kernel(xb_ref, x_ref, wg_ref, wu_ref, wd_ref, g_ref, b_ref, xf_ref, xo_ref, acc_ref, *, alpha):
    f = pl.program_id(1)

    @pl.when(f == 0)
    def _():
        acc_ref[...] = jnp.zeros_like(acc_ref)

    xb = xb_ref[...]
    h = jax.nn.silu(_dot(xb, wg_ref[...])) * _dot(xb, wu_ref[...])
    acc_ref[...] += _dot(h.astype(BF16), wd_ref[...])

    @pl.when(f == pl.num_programs(1) - 1)
    def _():
        y = _layer_norm(alpha * x_ref[...] + acc_ref[...], g_ref[...], b_ref[...])
        xf_ref[...] = y
        xo_ref[...] = y.astype(BF16)


def _ffn(xb, x, wg, wu, wd, li, g, b, alpha, tm, tf):
    T, D = x.shape
    F = wg.shape[2]
    row = pl.BlockSpec((tm, D), lambda i, f: (i, 0))
    vec = pl.BlockSpec((1, D), lambda i, f: (0, 0))
    return pl.pallas_call(
        functools.partial(_ffn_kernel, alpha=alpha),
        grid=(T // tm, F // tf),
        in_specs=[row, row,
                  pl.BlockSpec((None, D, tf), lambda i, f: (li, 0, f)),
                  pl.BlockSpec((None, D, tf), lambda i, f: (li, 0, f)),
                  pl.BlockSpec((None, tf, D), lambda i, f: (li, f, 0)),
                  vec, vec],
        out_specs=[row, row],
        out_shape=[jax.ShapeDtypeStruct((T, D), F32), jax.ShapeDtypeStruct((T, D), BF16)],
        scratch_shapes=[pltpu.VMEM((tm, D), F32)],
        compiler_params=_cparams(("parallel", "arbitrary")),
    )(xb, x, wg, wu, wd, g.reshape(1, D), b.reshape(1, D))


def _route(logits, tm):
    T = logits.shape[0]
    top_logits, top_idx = lax.top_k(logits[:, :N_EXPERTS], TOP_K)
    gates = jax.nn.softmax(top_logits, axis=-1)
    n_assign = T * TOP_K
    flat_e = top_idx.reshape(n_assign).astype(jnp.int32)
    onehot = (flat_e[:, None] == jnp.arange(N_EXPERTS, dtype=jnp.int32)[None, :]).astype(jnp.int32)
    csum = jnp.cumsum(onehot, axis=0)
    counts = csum[-1]
    rank = jnp.sum((csum - onehot) * onehot, axis=1)
    padded = (counts + tm - 1) // tm * tm
    pad_end = jnp.cumsum(padded)
    pad_start = pad_end - padded
    dest = pad_start[flat_e] + rank
    n_blocks = -(-(n_assign + N_EXPERTS * (tm - 1)) // tm)
    n_rows = n_blocks * tm
    row_assign = jnp.full((n_rows,), -1, jnp.int32).at[dest].set(jnp.arange(n_assign, dtype=jnp.int32))
    used = row_assign >= 0
    row_code = jnp.where(used, (row_assign // TOP_K) * 4 + (row_assign % TOP_K) * 2 + 1, 0)
    row_gate = jnp.where(used, gates.reshape(n_assign)[jnp.maximum(row_assign, 0)], 0.0)
    block_start = jnp.arange(n_blocks, dtype=jnp.int32) * tm
    block_expert = jnp.minimum(jnp.searchsorted(pad_end, block_start, side='right'), N_EXPERTS - 1).astype(jnp.int32)
    n_valid = (pad_end[-1] // tm).astype(jnp.int32).reshape(1)
    return block_expert, n_valid, row_code, jnp.broadcast_to(row_gate[:, None], (n_rows, LANES))


def _moe_kernel(be_ref, nv_ref, code_ref, x_hbm, gate_ref, wg_ref, wu_ref, wd_ref, y_hbm,
                xg_ref, xb_ref, acc_ref, yo_ref, sem_in, sem_out, *, tm, n_tok, chunk):
    i = pl.program_id(0)
    f = pl.program_id(1)
    n_f = pl.num_programs(1)
    n_blk = pl.num_programs(0)
    n_live = nv_ref[0]
    live = i < n_live
    slot = i % 2
    n_chunks = tm // chunk

    def gather_chunk(blk, buf, c):
        r0 = pl.multiple_of(c * chunk, chunk)
        for j in range(chunk):
            tok = code_ref[blk * tm + r0 + j] >> 2
            pltpu.make_async_copy(x_hbm.at[pl.ds(tok, 1)], xg_ref.at[buf, pl.ds(r0 + j, 1)], sem_in.at[buf]).start()

    def scatter_chunk(blk, c):
        r0 = pl.multiple_of(c * chunk, chunk)
        for j in range(chunk):
            code = code_ref[blk * tm + r0 + j]
            dst = jnp.where((code & 1) == 1, ((code >> 1) & 1) * n_tok + (code >> 2), TOP_K * n_tok + r0 + j)
            pltpu.make_async_copy(yo_ref.at[pl.ds(r0 + j, 1)], y_hbm.at[pl.ds(dst, 1)], sem_out).start()

    def wait_scatter():
        pltpu.make_async_copy(yo_ref, yo_ref, sem_out).wait()

    @pl.when((i == 0) & (f == 0))
    def _():
        def body(c, carry):
            gather_chunk(0, 0, c)
            return carry

        lax.fori_loop(0, n_chunks, body, 0)
        yo_ref[...] = jnp.zeros_like(yo_ref)
        spare = pltpu.make_async_copy(yo_ref, y_hbm.at[pl.ds(TOP_K * n_tok, tm)], sem_out)
        spare.start()
        spare.wait()

    @pl.when(live & (f == 0))
    def _():
        pltpu.make_async_copy(xg_ref.at[slot], xg_ref.at[slot], sem_in.at[slot]).wait()
        xb_ref[...] = xg_ref[slot].astype(BF16)
        acc_ref[...] = jnp.zeros_like(acc_ref)

    @pl.when((f < n_chunks) & (i + 1 < n_live))
    def _():
        gather_chunk(i + 1, 1 - slot, f)

    @pl.when((f < n_chunks) & (i >= 1) & (i <= n_live))
    def _():
        scatter_chunk(i - 1, f)

    @pl.when(live)
    def _():
        xb = xb_ref[...]
        h = jax.nn.silu(_dot(xb, wg_ref[0])) * _dot(xb, wu_ref[0])
        acc_ref[...] += _dot(h.astype(BF16), wd_ref[0])

    @pl.when((f == n_f - 1) & (i >= 1) & (i <= n_live))
    def _():
        wait_scatter()

    @pl.when(live & (f == n_f - 1))
    def _():
        gate = gate_ref[...]
        for c in range(acc_ref.shape[1] // LANES):
            sl = slice(c * LANES, (c + 1) * LANES)
            yo_ref[:, sl] = acc_ref[:, sl] * gate

        @pl.when(i == n_blk - 1)
        def _():
            def body(c, carry):
                scatter_chunk(i, c)
                return carry

            lax.fori_loop(0, n_chunks, body, 0)
            wait_scatter()


def _moe(x, route, wg, wu, wd, li, tm, tf):
    T, D = x.shape
    F = wg.shape[2]
    nf = F // tf
    block_expert, n_valid, row_code, row_gate = route
    n_blocks = block_expert.shape[0]
    chunk = next(c for c in range(8, tm + 1, 8) if tm % c == 0 and tm // c <= nf)

    def fidx(i, f, nv):
        return jnp.where(i < nv[0], f, nf - 1)

    grid_spec = pltpu.PrefetchScalarGridSpec(
        num_scalar_prefetch=3,
        grid=(n_blocks, nf),
        in_specs=[
            pl.BlockSpec(memory_space=pl.ANY),
            pl.BlockSpec((tm, LANES), lambda i, f, be, nv, rc: (i, 0)),
            pl.BlockSpec((1, D, tf), lambda i, f, be, nv, rc: (li * N_EXPERTS + be[i], 0, fidx(i, f, nv))),
            pl.BlockSpec((1, D, tf), lambda i, f, be, nv, rc: (li * N_EXPERTS + be[i], 0, fidx(i, f, nv))),
            pl.BlockSpec((1, tf, D), lambda i, f, be, nv, rc: (li * N_EXPERTS + be[i], fidx(i, f, nv), 0)),
        ],
        out_specs=pl.BlockSpec(memory_space=pl.ANY),
        scratch_shapes=[pltpu.VMEM((2, tm, D), F32), pltpu.VMEM((tm, D), BF16), pltpu.VMEM((tm, D), F32),
                        pltpu.VMEM((tm, D), F32), pltpu.SemaphoreType.DMA((2,)), pltpu.SemaphoreType.DMA(())],
    )
    y = pl.pallas_call(
        functools.partial(_moe_kernel, tm=tm, n_tok=T, chunk=chunk),
        grid_spec=grid_spec,
        out_shape=jax.ShapeDtypeStruct((TOP_K * T + tm, D), F32),
        compiler_params=_cparams(("arbitrary", "arbitrary")),
    )(block_expert, n_valid, row_code, x, row_gate, wg, wu, wd)
    return y


def _combine_kernel(x_ref, y0_ref, y1_ref, g_ref, b_ref, xf_ref, *rest, alpha):
    y = _layer_norm(alpha * x_ref[...] + (y0_ref[...] + y1_ref[...]), g_ref[...], b_ref[...])
    xf_ref[...] = y
    for xb_ref in rest:
        xb_ref[...] = y.astype(BF16)


def _combine(x, y, g, b, alpha, tm, row0, n_rows, with_bf16):
    T, D = x.shape
    n = T // tm
    i0 = row0 // tm
    row = pl.BlockSpec((tm, D), lambda i: (i, 0))
    vec = pl.BlockSpec((1, D), lambda i: (0, 0))
    out_specs = [row, row] if with_bf16 else [row]
    out_shape = [jax.ShapeDtypeStruct((n_rows, D), F32)]
    if with_bf16:
        out_shape.append(jax.ShapeDtypeStruct((n_rows, D), BF16))
    return pl.pallas_call(
        functools.partial(_combine_kernel, alpha=alpha),
        grid=(n_rows // tm,),
        in_specs=[pl.BlockSpec((tm, D), lambda i: (i0 + i, 0)),
                  pl.BlockSpec((tm, D), lambda i: (i0 + i, 0)),
                  pl.BlockSpec((tm, D), lambda i: (n + i0 + i, 0)), vec, vec],
        out_specs=out_specs,
        out_shape=out_shape,
        compiler_params=_cparams(("parallel",)),
    )(x, y, y, g.reshape(1, D), b.reshape(1, D))


def _tiles(D):
    big = D >= 2048
    return dict(ln=512, proj=1024, attn_b=256, post=256 if big else 512, ffn_m=512, ffn_f=512,
                moe_m=512, moe_f=512, comb=512)


def kernel(x_prompt, x_sample, ln_in_g, ln_in_b, w_in, w_branch_a, w_branch_b, w_out, diff_lambda, diff_subln_g, ln_mix_g, ln_mix_b, ln_ffn_g, ln_ffn_b, ffn_w_gate, ffn_w_up, ffn_w_down, moe_router, moe_w_gate, moe_w_up, moe_w_down):
    depth = w_in.shape[0]
    D = x_prompt.shape[-1]
    alpha = (2 * depth) ** 0.25
    tl = _tiles(D)
    seqs = [(x_prompt.shape[0], x_prompt.shape[1]), (x_sample.shape[0], x_sample.shape[1])]
    x = jnp.concatenate([x_prompt.reshape(-1, D), x_sample.reshape(-1, D)], axis=0)
    T = x.shape[0]
    tab = _rope_tables(max(S for _, S in seqs))
    bf = lambda w: w.astype(BF16)
    w_in, w_branch_a, w_branch_b, w_out = bf(w_in), bf(w_branch_a), bf(w_branch_b), bf(w_out)
    ffn_w_gate, ffn_w_up, ffn_w_down = bf(ffn_w_gate), bf(ffn_w_up), bf(ffn_w_down)
    merge = lambda w: bf(w).reshape((-1,) + w.shape[2:])
    moe_w_gate, moe_w_up, moe_w_down = merge(moe_w_gate), merge(moe_w_up), merge(moe_w_down)

    xf, xb = _ln_in(x, ln_in_g, ln_in_b, tl['ln'])
    for l in range(depth):
        lam_init = 0.8 - 0.6 * math.exp(-0.3 * l)
        consts = jnp.full((1, LANES), lam_init, F32)
        proj = _proj(xb, w_in, l, tab, seqs, tl['proj'])

        oa = []
        for g in range(len(A_GROUPS)):
            prev, row0 = None, 0
            for n_seq, S in seqs:
                prev = _attn_a_group(proj, prev, g, n_seq, S, row0, T)
                row0 += n_seq * S
            oa.append(prev)
        ob, row0 = None, 0
        for n_seq, S in seqs:
            ob = _attn_b(proj, ob, diff_lambda[l], diff_subln_g[l], consts, n_seq, S, row0, T, tl['attn_b'])
            row0 += n_seq * S

        moe_layer = l % 2 == 1
        i = l // 2
        wr = None
        if moe_layer:
            wr = jnp.zeros((D, LANES), F32).at[:, :N_EXPERTS].set(moe_router[i])
        res = _post(oa, ob, proj, xf, w_branch_a, w_branch_b, w_out, l,
                    ln_mix_g[l], ln_mix_b[l], wr, alpha, tl['post'])
        if moe_layer:
            xf, xb, logits = res
            route = _route(logits, tl['moe_m'])
            y = _moe(xf, route, moe_w_gate, moe_w_up, moe_w_down, i, tl['moe_m'], tl['moe_f'])
            if l == depth - 1:
                n_p = seqs[0][0] * seqs[0][1]
                outs = [_combine(xf, y, ln_ffn_g[l], ln_ffn_b[l], alpha, tl['comb'], r0, n, False)[0]
                        for r0, n in ((0, n_p), (n_p, T - n_p))]
                return (outs[0].reshape(x_prompt.shape), outs[1].reshape(x_sample.shape))
            xf, xb = _combine(xf, y, ln_ffn_g[l], ln_ffn_b[l], alpha, tl['comb'], 0, T, True)
        else:
            xf, xb = res
            xf, xb = _ffn(xb, xf, ffn_w_gate, ffn_w_up, ffn_w_down, i,
                          ln_ffn_g[l], ln_ffn_b[l], alpha, tl['ffn_m'], tl['ffn_f'])

    n_p = seqs[0][0] * seqs[0][1]
    return (xf[:n_p].reshape(x_prompt.shape), xf[n_p:].reshape(x_sample.shape))
```

```python
import functools
import math

import jax
import jax.numpy as jnp
from jax import lax
from jax.experimental import pallas as pl
from jax.experimental.pallas import tpu as pltpu

F32 = jnp.float32
BF16 = jnp.bfloat16

HEAD_DIM = 128
A_GROUPS = ((128, 1), (512, 4), (2048, 16))
A_HEADS_PER_GROUP = 4
A_OUT = A_HEADS_PER_GROUP * HEAD_DIM
A_QKV = len(A_GROUPS) * A_OUT
B_HEADS = 4
B_QK_DIM = 64
B_V_DIM = 2 * B_QK_DIM
B_COLS = B_HEADS * B_V_DIM
ROPE_THETA = 500000.0
ROPE_FRACTION = 4
N_EXPERTS = 8
TOP_K = 2
LN_EPS = 1e-5
MASK_VALUE = -1e30
LOG2_E = math.log2(math.e)
GATE_COL0 = 3 * A_QKV + 3 * B_COLS

LANES = 128
COL_BLOCK = 512
VMEM_LIMIT = 56 * 1024 * 1024


def _cparams(sem):
    return pltpu.CompilerParams(dimension_semantics=sem, vmem_limit_bytes=VMEM_LIMIT)


def _layer_norm(y, g, b):
    mu = jnp.mean(y, axis=-1, keepdims=True)
    yc = y - mu
    var = jnp.mean(yc * yc, axis=-1, keepdims=True)
    return yc * lax.rsqrt(var + LN_EPS) * g + b


def _dot(a, b):
    return jnp.dot(a, b, preferred_element_type=F32)


def _dot_nt(a, b):
    return lax.dot_general(a, b, (((1,), (1,)), ((), ())), preferred_element_type=F32)


def _ln_in_kernel(x_ref, g_ref, b_ref, xf_ref, xb_ref):
    y = _layer_norm(x_ref[...], g_ref[...], b_ref[...])
    xf_ref[...] = y
    xb_ref[...] = y.astype(BF16)


def _ln_in(x, g, b, tm):
    T, D = x.shape
    row = pl.BlockSpec((tm, D), lambda i: (i, 0))
    vec = pl.BlockSpec((1, D), lambda i: (0, 0))
    return pl.pallas_call(
        _ln_in_kernel,
        grid=(T // tm,),
        in_specs=[row, vec, vec],
        out_specs=[row, row],
        out_shape=[jax.ShapeDtypeStruct((T, D), F32), jax.ShapeDtypeStruct((T, D), BF16)],
        compiler_params=_cparams(("parallel",)),
    )(x, g.reshape(1, D), b.reshape(1, D))


def _rope_tables(s_max):
    pos = jnp.arange(s_max, dtype=F32)[:, None]

    def tables(head, n_rep):
        rot = head // ROPE_FRACTION
        half = rot // 2
        inv_freq = ROPE_THETA ** (-jnp.arange(half, dtype=F32) / half)
        ang = pos * inv_freq[None, :]
        cos, sin = jnp.cos(ang), jnp.sin(ang)
        zero = jnp.zeros((s_max, head - rot), F32)
        zh = jnp.zeros((s_max, half), F32)
        c = jnp.concatenate([cos, cos, jnp.ones((s_max, head - rot), F32)], axis=1)
        s_up = jnp.concatenate([-sin, zh, zero], axis=1)
        s_dn = jnp.concatenate([zh, sin, zero], axis=1)
        return [jnp.tile(t, (1, n_rep)) for t in (c, s_up, s_dn)]

    return jnp.stack(tables(HEAD_DIM, 1) + tables(B_QK_DIM, 2), axis=0)


def _proj_kernel(x_ref, w_ref, tab_ref, o_ref):
    j = pl.program_id(1)
    n_a = A_QKV // COL_BLOCK
    is_a = j < 2 * n_a
    jb = j - 3 * n_a
    is_b = (jb == 0) | (jb == 1)

    def rope(t0, half, scale):
        acc = _dot(x_ref[...], w_ref[...])
        for c in range(COL_BLOCK // LANES):
            sl = slice(c * LANES, (c + 1) * LANES)
            xc = acc[:, sl]
            up = pltpu.roll(xc, LANES - half, 1)
            dn = pltpu.roll(xc, half, 1)
            y = xc * tab_ref[t0] + up * tab_ref[t0 + 1] + dn * tab_ref[t0 + 2]
            o_ref[:, sl] = (y * scale).astype(BF16)

    @pl.when(is_a)
    def _():
        rope(0, HEAD_DIM // ROPE_FRACTION // 2, jnp.where(j < n_a, HEAD_DIM ** -0.5, 1.0).astype(F32))

    @pl.when(is_b)
    def _():
        rope(3, B_QK_DIM // ROPE_FRACTION // 2, jnp.where(jb == 0, B_QK_DIM ** -0.5 * LOG2_E, 1.0).astype(F32))

    @pl.when(jnp.logical_not(is_a | is_b))
    def _():
        o_ref[...] = _dot(x_ref[...], w_ref[...]).astype(BF16)


def _proj(xb, w, l, tab, seqs, tm):
    T, D = xb.shape
    C = w.shape[2]
    bounds = []
    t0 = 0
    for n_seq, S in seqs:
        assert S % tm == 0
        bounds.append((t0 // tm, S // tm))
        t0 += n_seq * S
    assert t0 == T

    def pos_block(i):
        blk = (i - bounds[0][0]) % bounds[0][1]
        for first, per in bounds[1:]:
            blk = jnp.where(i >= first, (i - first) % per, blk)
        return blk

    return pl.pallas_call(
        _proj_kernel,
        grid=(T // tm, C // COL_BLOCK),
        in_specs=[
            pl.BlockSpec((tm, D), lambda i, j: (i, 0)),
            pl.BlockSpec((None, D, COL_BLOCK), lambda i, j: (l, 0, j)),
            pl.BlockSpec((6, tm, LANES), lambda i, j: (0, pos_block(i), 0)),
        ],
        out_specs=pl.BlockSpec((tm, COL_BLOCK), lambda i, j: (i, j)),
        out_shape=jax.ShapeDtypeStruct((T, C), BF16),
        compiler_params=_cparams(("parallel", "arbitrary")),
    )(xb, w, tab)


def _attn_a_kernel(q_ref, k_ref, v_ref, *rest, L, tq, kw, n_side):
    o_ref, lse_ref = rest[-2], rest[-1]
    n_q = L // tq

    def body(qi, carry):
        q0 = pl.multiple_of(qi * tq, tq)
        start = pl.multiple_of(jnp.clip(q0 - n_side, 0, L - kw), n_side)
        qpos = q0 + lax.broadcasted_iota(jnp.int32, (tq, 1), 0)
        kpos = start + lax.broadcasted_iota(jnp.int32, (1, kw), 1)
        mask = jnp.abs(kpos - qpos) <= n_side
        for h in range(A_HEADS_PER_GROUP):
            sl = slice(h * HEAD_DIM, (h + 1) * HEAD_DIM)
            q = q_ref[pl.ds(q0, tq), sl]
            k = k_ref[pl.ds(start, kw), sl]
            v = v_ref[pl.ds(start, kw), sl]
            s = jnp.where(mask, _dot_nt(q, k), MASK_VALUE)
            m = jnp.max(s, axis=-1, keepdims=True)
            p = jnp.exp(s - m)
            den = jnp.sum(p, axis=-1, keepdims=True)
            o = _dot(p.astype(BF16), v) / den
            o_ref[pl.ds(q0, tq), sl] = o.astype(BF16)
            lse_ref[pl.ds(q0, tq), sl] = jnp.broadcast_to(m + jnp.log(den), (tq, HEAD_DIM))
        return carry

    lax.fori_loop(0, n_q, body, 0)


def _attn_a_dil_kernel(q_ref, k_ref, v_ref, *rest, S, d, tq, kw, n_side, unroll):
    o_ref, lse_ref, qf_ref, kf_ref, vf_ref = rest[-5:]
    L = S // d
    qf_ref[...] = q_ref[...].astype(F32)
    kf_ref[...] = k_ref[...].astype(F32)
    vf_ref[...] = v_ref[...].astype(F32)

    def body(r, carry):
        for qi in range(L // tq):
            q0 = qi * tq
            start = min(max(q0 - n_side, 0), L - kw)
            qpos = q0 + lax.broadcasted_iota(jnp.int32, (tq, 1), 0)
            kpos = start + lax.broadcasted_iota(jnp.int32, (1, kw), 1)
            mask = jnp.abs(kpos - qpos) <= n_side
            q = qf_ref[pl.ds(r + q0 * d, tq, stride=d), :].astype(BF16)
            k = kf_ref[pl.ds(r + start * d, kw, stride=d), :].astype(BF16)
            v = vf_ref[pl.ds(r + start * d, kw, stride=d), :].astype(BF16)
            s = jnp.where(mask, _dot_nt(q, k), MASK_VALUE)
            m = jnp.max(s, axis=-1, keepdims=True)
            p = jnp.exp(s - m)
            den = jnp.sum(p, axis=-1, keepdims=True)
            o_ref[pl.ds(r + q0 * d, tq, stride=d), :] = _dot(p.astype(BF16), v) / den
            lse_ref[pl.ds(r + q0 * d, tq, stride=d), :] = jnp.broadcast_to(m + jnp.log(den), (tq, HEAD_DIM))
        return carry

    lax.fori_loop(0, d, body, 0, unroll=unroll)


def _attn_a_group(proj, prev, g, n_seq, S, row0, T):
    window, d = A_GROUPS[g]
    n_side = window // (2 * d)
    L = S // d
    tq = min(128, L)
    kw = min(L, tq + 2 * n_side)
    assert L % tq == 0 and tq % n_side == 0 and (L - kw) % n_side == 0
    n_g = len(A_GROUPS)
    blk0 = row0 // S
    o_dtype = BF16 if d == 1 else F32
    if prev is None:
        prev = (jnp.zeros((T, A_OUT), o_dtype), jnp.zeros((T, A_OUT), F32))
    hbm = pl.BlockSpec(memory_space=pl.ANY)
    out_shape = [jax.ShapeDtypeStruct((T, A_OUT), o_dtype), jax.ShapeDtypeStruct((T, A_OUT), F32)]
    if d == 1:
        spec = lambda part: pl.BlockSpec((S, COL_BLOCK), lambda b: (blk0 + b, part * n_g + g))
        out_spec = pl.BlockSpec((S, A_OUT), lambda b: (blk0 + b, 0))
        return pl.pallas_call(
            functools.partial(_attn_a_kernel, L=L, tq=tq, kw=kw, n_side=n_side),
            grid=(n_seq,),
            in_specs=[spec(0), spec(1), spec(2), hbm, hbm],
            out_specs=[out_spec, out_spec],
            out_shape=out_shape,
            input_output_aliases={3: 0, 4: 1},
            compiler_params=_cparams(("parallel",)),
        )(proj, proj, proj, *prev)
    heads = n_g * A_HEADS_PER_GROUP
    spec = lambda part: pl.BlockSpec((S, HEAD_DIM), lambda b, h: (blk0 + b, part * heads + g * A_HEADS_PER_GROUP + h))
    out_spec = pl.BlockSpec((S, HEAD_DIM), lambda b, h: (blk0 + b, h))
    unroll = max(1, min(d, 8 // (L // tq)))
    return pl.pallas_call(
        functools.partial(_attn_a_dil_kernel, S=S, d=d, tq=tq, kw=kw, n_side=n_side, unroll=unroll),
        grid=(n_seq, A_HEADS_PER_GROUP),
        in_specs=[spec(0), spec(1), spec(2), hbm, hbm],
        out_specs=[out_spec, out_spec],
        out_shape=out_shape,
        input_output_aliases={3: 0, 4: 1},
        scratch_shapes=[pltpu.VMEM((S, HEAD_DIM), F32)] * 3,
        compiler_params=_cparams(("parallel", "parallel")),
    )(proj, proj, proj, *prev)


def _attn_b_kernel(q_ref, k_ref, v_ref, lam_ref, g_ref, c_ref, *rest):
    o_ref = rest[-1]
    q = q_ref[...]
    k = k_ref[...]
    v = v_ref[...]
    lane = lax.broadcasted_iota(jnp.int32, (1, 2 * B_QK_DIM), 1)
    zero = jnp.zeros_like(q)
    lam_init = c_ref[0:1, 0:1]
    lv = lam_ref[...]
    lam = (jnp.exp(jnp.sum(lv[0:1] * lv[1:2], axis=-1, keepdims=True))
           - jnp.exp(jnp.sum(lv[2:3] * lv[3:4], axis=-1, keepdims=True)) + lam_init)

    def softmax(qz):
        s = _dot_nt(qz, k)
        p = jnp.exp2(s - jnp.max(s, axis=-1, keepdims=True))
        return p, 1.0 / jnp.sum(p, axis=-1, keepdims=True)

    p1, r1 = softmax(jnp.where(lane < B_QK_DIM, q, zero))
    p2, r2 = softmax(jnp.where(lane >= B_QK_DIM, q, zero))
    o = _dot((p1 * r1 - p2 * (lam * r2)).astype(BF16), v)
    o = o * lax.rsqrt(jnp.mean(o * o, axis=-1, keepdims=True) + LN_EPS) * g_ref[...] * (1.0 - lam_init)
    o_ref[...] = o.astype(BF16)


def _attn_b(proj, prev, lam_vecs, subln_g, consts, n_seq, S, row0, T, tq):
    C = proj.shape[1]
    blk0 = row0 // S
    qb0 = row0 // tq
    n_qt = S // tq
    col = lambda part: (3 * A_QKV + part * B_COLS) // LANES
    in_specs = [
        pl.BlockSpec((tq, LANES), lambda b, h, i: (qb0 + b * n_qt + i, col(0) + h)),
        pl.BlockSpec((S, LANES), lambda b, h, i: (blk0 + b, col(1) + h)),
        pl.BlockSpec((S, LANES), lambda b, h, i: (blk0 + b, col(2) + h)),
        pl.BlockSpec((4, B_QK_DIM), lambda b, h, i: (0, 0)),
        pl.BlockSpec((1, B_V_DIM), lambda b, h, i: (0, 0)),
        pl.BlockSpec((1, LANES), lambda b, h, i: (0, 0)),
    ]
    if prev is None:
        prev = jnp.zeros((T, B_COLS), BF16)
    in_specs.append(pl.BlockSpec(memory_space=pl.ANY))
    args = [proj, proj, proj, lam_vecs, subln_g.reshape(1, B_V_DIM), consts, prev]
    aliases = {6: 0}
    return pl.pallas_call(
        _attn_b_kernel,
        grid=(n_seq, B_HEADS, n_qt),
        in_specs=in_specs,
        out_specs=pl.BlockSpec((tq, LANES), lambda b, h, i: (qb0 + b * n_qt + i, h)),
        out_shape=jax.ShapeDtypeStruct((T, B_COLS), BF16),
        input_output_aliases=aliases,
        compiler_params=_cparams(("parallel", "parallel", "arbitrary")),
    )(*args)


def _post_kernel(o1_ref, o2_ref, o3_ref, l1_ref, l2_ref, l3_ref, ob_ref, ga_ref, gb_ref, x_ref,
                 wa_ref, wb_ref, wo_ref, g_ref, b_ref, *rest, alpha, with_router):
    if with_router:
        wrh_ref, wrl_ref, xf_ref, xb_ref, lg_ref = rest
    else:
        xf_ref, xb_ref = rest
    l1, l2, l3 = l1_ref[...], l2_ref[...], l3_ref[...]
    m = jnp.maximum(jnp.maximum(l1, l2), l3)
    e1, e2, e3 = jnp.exp(l1 - m), jnp.exp(l2 - m), jnp.exp(l3 - m)
    oa = (e1 * o1_ref[...].astype(F32) + e2 * o2_ref[...].astype(F32) + e3 * o3_ref[...].astype(F32)) / (e1 + e2 + e3)
    ya = _dot(oa.astype(BF16), wa_ref[...])
    yb = _dot(ob_ref[...], wb_ref[...])
    mix = jax.nn.sigmoid(ga_ref[...].astype(F32)) * ya + jax.nn.sigmoid(gb_ref[...].astype(F32)) * yb
    z = _dot(mix.astype(BF16), wo_ref[...])
    y = _layer_norm(alpha * x_ref[...] + z, g_ref[...], b_ref[...])
    yh = y.astype(BF16)
    xf_ref[...] = y
    xb_ref[...] = yh
    if with_router:
        yl = (y - yh.astype(F32)).astype(BF16)
        lg_ref[...] = _dot(yh, wrh_ref[...]) + (_dot(yl, wrh_ref[...]) + _dot(yh, wrl_ref[...]))


def _post(oa, ob, proj, x, wa, wb, wo, l, g, b, wr, alpha, tm):
    T, D = x.shape
    assert GATE_COL0 % D == 0
    gcol = GATE_COL0 // D
    row = lambda w: pl.BlockSpec((tm, w), lambda i: (i, 0))
    full = lambda a: pl.BlockSpec(a.shape, lambda i: (0,) * a.ndim)
    layer = lambda a: pl.BlockSpec((None,) + a.shape[1:], lambda i: (l, 0, 0))
    g2, b2 = g.reshape(1, D), b.reshape(1, D)
    (o1, l1), (o2, l2), (o3, l3) = oa
    in_specs = [row(A_OUT)] * 6 + [row(B_COLS),
                                   pl.BlockSpec((tm, D), lambda i: (i, gcol)),
                                   pl.BlockSpec((tm, D), lambda i: (i, gcol + 1)),
                                   row(D), layer(wa), layer(wb), layer(wo), full(g2), full(b2)]
    args = [o1, o2, o3, l1, l2, l3, ob, proj, proj, x, wa, wb, wo, g2, b2]
    out_specs = [row(D), row(D)]
    out_shape = [jax.ShapeDtypeStruct((T, D), F32), jax.ShapeDtypeStruct((T, D), BF16)]
    if wr is not None:
        wr_hi = wr.astype(BF16)
        wr_lo = (wr - wr_hi.astype(F32)).astype(BF16)
        in_specs += [full(wr_hi), full(wr_lo)]
        args += [wr_hi, wr_lo]
        out_specs.append(row(LANES))
        out_shape.append(jax.ShapeDtypeStruct((T, LANES), F32))
    return pl.pallas_call(
        functools.partial(_post_kernel, alpha=alpha, with_router=wr is not None),
        grid=(T // tm,),
        in_specs=in_specs,
        out_specs=out_specs,
        out_shape=out_shape,
        compiler_params=_cparams(("parallel",)),
    )(*args)


def _ffn_kernel(xb_ref, x_ref, wg_ref, wu_ref, wd_ref, g_ref, b_ref, xf_ref, xo_ref, acc_ref, *, alpha):
    f = pl.program_id(1)

    @pl.when(f == 0)
    def _():
        acc_ref[...] = jnp.zeros_like(acc_ref)

    xb = xb_ref[...]
    h = jax.nn.silu(_dot(xb, wg_ref[...])) * _dot(xb, wu_ref[...])
    acc_ref[...] += _dot(h.astype(BF16), wd_ref[...])

    @pl.when(f == pl.num_programs(1) - 1)
    def _():
        y = _layer_norm(alpha * x_ref[...] + acc_ref[...], g_ref[...], b_ref[...])
        xf_ref[...] = y
        xo_ref[...] = y.astype(BF16)


def _ffn(xb, x, wg, wu, wd, li, g, b, alpha, tm, tf):
    T, D = x.shape
    F = wg.shape[2]
    row = pl.BlockSpec((tm, D), lambda i, f: (i, 0))
    vec = pl.BlockSpec((1, D), lambda i, f: (0, 0))
    return pl.pallas_call(
        functools.partial(_ffn_kernel, alpha=alpha),
        grid=(T // tm, F // tf),
        in_specs=[row, row,
                  pl.BlockSpec((None, D, tf), lambda i, f: (li, 0, f)),
                  pl.BlockSpec((None, D, tf), lambda i, f: (li, 0, f)),
                  pl.BlockSpec((None, tf, D), lambda i, f: (li, f, 0)),
                  vec, vec],
        out_specs=[row, row],
        out_shape=[jax.ShapeDtypeStruct((T, D), F32), jax.ShapeDtypeStruct((T, D), BF16)],
        scratch_shapes=[pltpu.VMEM((tm, D), F32)],
        compiler_params=_cparams(("parallel", "arbitrary")),
    )(xb, x, wg, wu, wd, g.reshape(1, D), b.reshape(1, D))


def _route(logits, tm):
    T = logits.shape[0]
    top_logits, top_idx = lax.top_k(logits[:, :N_EXPERTS], TOP_K)
    gates = jax.nn.softmax(top_logits, axis=-1)
    n_assign = T * TOP_K
    flat_e = top_idx.reshape(n_assign).astype(jnp.int32)
    onehot = (flat_e[:, None] == jnp.arange(N_EXPERTS, dtype=jnp.int32)[None, :]).astype(jnp.int32)
    csum = jnp.cumsum(onehot, axis=0)
    counts = csum[-1]
    rank = jnp.sum((csum - onehot) * onehot, axis=1)
    padded = (counts + tm - 1) // tm * tm
    pad_end = jnp.cumsum(padded)
    pad_start = pad_end - padded
    dest = pad_start[flat_e] + rank
    n_blocks = -(-(n_assign + N_EXPERTS * (tm - 1)) // tm)
    n_rows = n_blocks * tm
    row_assign = jnp.full((n_rows,), -1, jnp.int32).at[dest].set(jnp.arange(n_assign, dtype=jnp.int32))
    used = row_assign >= 0
    row_code = jnp.where(used, (row_assign // TOP_K) * 4 + (row_assign % TOP_K) * 2 + 1, 0)
    row_gate = jnp.where(used, gates.reshape(n_assign)[jnp.maximum(row_assign, 0)], 0.0)
    block_start = jnp.arange(n_blocks, dtype=jnp.int32) * tm
    block_expert = jnp.minimum(jnp.searchsorted(pad_end, block_start, side='right'), N_EXPERTS - 1).astype(jnp.int32)
    n_valid = (pad_end[-1] // tm).astype(jnp.int32).reshape(1)
    return block_expert, n_valid, row_code, jnp.broadcast_to(row_gate[:, None], (n_rows, LANES))


def _moe_kernel(be_ref, nv_ref, code_ref, x_hbm, gate_ref, wg_ref, wu_ref, wd_ref, y_hbm,
                xg_ref, xb_ref, acc_ref, yo_ref, sem_in, sem_out, *, tm, n_tok, chunk):
    i = pl.program_id(0)
    f = pl.program_id(1)
    n_f = pl.num_programs(1)
    n_blk = pl.num_programs(0)
    n_live = nv_ref[0]
    live = i < n_live
    slot = i % 2
    n_chunks = tm // chunk

    def gather_chunk(blk, buf, c):
        r0 = pl.multiple_of(c * chunk, chunk)
        for j in range(chunk):
            tok = code_ref[blk * tm + r0 + j] >> 2
            pltpu.make_async_copy(x_hbm.at[pl.ds(tok, 1)], xg_ref.at[buf, pl.ds(r0 + j, 1)], sem_in.at[buf]).start()

    def scatter_chunk(blk, c):
        r0 = pl.multiple_of(c * chunk, chunk)
        for j in range(chunk):
            code = code_ref[blk * tm + r0 + j]
            dst = jnp.where((code & 1) == 1, ((code >> 1) & 1) * n_tok + (code >> 2), TOP_K * n_tok + r0 + j)
            pltpu.make_async_copy(yo_ref.at[pl.ds(r0 + j, 1)], y_hbm.at[pl.ds(dst, 1)], sem_out).start()

    def wait_scatter():
        pltpu.make_async_copy(yo_ref, yo_ref, sem_out).wait()

    @pl.when((i == 0) & (f == 0))
    def _():
        def body(c, carry):
            gather_chunk(0, 0, c)
            return carry

        lax.fori_loop(0, n_chunks, body, 0)
        yo_ref[...] = jnp.zeros_like(yo_ref)
        spare = pltpu.make_async_copy(yo_ref, y_hbm.at[pl.ds(TOP_K * n_tok, tm)], sem_out)
        spare.start()
        spare.wait()

    @pl.when(live & (f == 0))
    def _():
        pltpu.make_async_copy(xg_ref.at[slot], xg_ref.at[slot], sem_in.at[slot]).wait()
        xb_ref[...] = xg_ref[slot].astype(BF16)
        acc_ref[...] = jnp.zeros_like(acc_ref)

    @pl.when((f < n_chunks) & (i + 1 < n_live))
    def _():
        gather_chunk(i + 1, 1 - slot, f)

    @pl.when((f < n_chunks) & (i >= 1) & (i <= n_live))
    def _():
        scatter_chunk(i - 1, f)

    @pl.when(live)
    def _():
        xb = xb_ref[...]
        h = jax.nn.silu(_dot(xb, wg_ref[0])) * _dot(xb, wu_ref[0])
        acc_ref[...] += _dot(h.astype(BF16), wd_ref[0])

    @pl.when((f == n_f - 1) & (i >= 1) & (i <= n_live))
    def _():
        wait_scatter()

    @pl.when(live & (f == n_f - 1))
    def _():
        gate = gate_ref[...]
        for c in range(acc_ref.shape[1] // LANES):
            sl = slice(c * LANES, (c + 1) * LANES)
            yo_ref[:, sl] = acc_ref[:, sl] * gate

        @pl.when(i == n_blk - 1)
        def _():
            def body(c, carry):
                scatter_chunk(i, c)
                return carry

            lax.fori_loop(0, n_chunks, body, 0)
            wait_scatter()


def _moe(x, route, wg, wu, wd, li, tm, tf):
    T, D = x.shape
    F = wg.shape[2]
    nf = F // tf
    block_expert, n_valid, row_code, row_gate = route
    n_blocks = block_expert.shape[0]
    chunk = next(c for c in range(8, tm + 1, 8) if tm % c == 0 and tm // c <= nf)

    def fidx(i, f, nv):
        return jnp.where(i < nv[0], f, nf - 1)

    grid_spec = pltpu.PrefetchScalarGridSpec(
        num_scalar_prefetch=3,
        grid=(n_blocks, nf),
        in_specs=[
            pl.BlockSpec(memory_space=pl.ANY),
            pl.BlockSpec((tm, LANES), lambda i, f, be, nv, rc: (i, 0)),
            pl.BlockSpec((1, D, tf), lambda i, f, be, nv, rc: (li * N_EXPERTS + be[i], 0, fidx(i, f, nv))),
            pl.BlockSpec((1, D, tf), lambda i, f, be, nv, rc: (li * N_EXPERTS + be[i], 0, fidx(i, f, nv))),
            pl.BlockSpec((1, tf, D), lambda i, f, be, nv, rc: (li * N_EXPERTS + be[i], fidx(i, f, nv), 0)),
        ],
        out_specs=pl.BlockSpec(memory_space=pl.ANY),
        scratch_shapes=[pltpu.VMEM((2, tm, D), F32), pltpu.VMEM((tm, D), BF16), pltpu.VMEM((tm, D), F32),
                        pltpu.VMEM((tm, D), F32), pltpu.SemaphoreType.DMA((2,)), pltpu.SemaphoreType.DMA(())],
    )
    y = pl.pallas_call(
        functools.partial(_moe_kernel, tm=tm, n_tok=T, chunk=chunk),
        grid_spec=grid_spec,
        out_shape=jax.ShapeDtypeStruct((TOP_K * T + tm, D), F32),
        compiler_params=_cparams(("arbitrary", "arbitrary")),
    )(block_expert, n_valid, row_code, x, row_gate, wg, wu, wd)
    return y


def _combine_kernel(x_ref, y0_ref, y1_ref, g_ref, b_ref, xf_ref, *rest, alpha):
    y = _layer_norm(alpha * x_ref[...] + (y0_ref[...] + y1_ref[...]), g_ref[...], b_ref[...])
    xf_ref[...] = y
    for xb_ref in rest:
        xb_ref[...] = y.astype(BF16)


def _combine(x, y, g, b, alpha, tm, row0, n_rows, with_bf16):
    T, D = x.shape
    n = T // tm
    i0 = row0 // tm
    row = pl.BlockSpec((tm, D), lambda i: (i, 0))
    vec = pl.BlockSpec((1, D), lambda i: (0, 0))
    out_specs = [row, row] if with_bf16 else [row]
    out_shape = [jax.ShapeDtypeStruct((n_rows, D), F32)]
    if with_bf16:
        out_shape.append(jax.ShapeDtypeStruct((n_rows, D), BF16))
    return pl.pallas_call(
        functools.partial(_combine_kernel, alpha=alpha),
        grid=(n_rows // tm,),
        in_specs=[pl.BlockSpec((tm, D), lambda i: (i0 + i, 0)),
                  pl.BlockSpec((tm, D), lambda i: (i0 + i, 0)),
                  pl.BlockSpec((tm, D), lambda i: (n + i0 + i, 0)), vec, vec],
        out_specs=out_specs,
        out_shape=out_shape,
        compiler_params=_cparams(("parallel",)),
    )(x, y, y, g.reshape(1, D), b.reshape(1, D))


def _tiles(D):
    big = D >= 2048
    return dict(ln=512, proj=1024, attn_b=256, post=256 if big else 512, ffn_m=512, ffn_f=512,
                moe_m=512, moe_f=1024, comb=512)


def kernel(x_prompt, x_sample, ln_in_g, ln_in_b, w_in, w_branch_a, w_branch_b, w_out, diff_lambda, diff_subln_g, ln_mix_g, ln_mix_b, ln_ffn_g, ln_ffn_b, ffn_w_gate, ffn_w_up, ffn_w_down, moe_router, moe_w_gate, moe_w_up, moe_w_down):
    depth = w_in.shape[0]
    D = x_prompt.shape[-1]
    alpha = (2 * depth) ** 0.25
    tl = _tiles(D)
    seqs = [(x_prompt.shape[0], x_prompt.shape[1]), (x_sample.shape[0], x_sample.shape[1])]
    x = jnp.concatenate([x_prompt.reshape(-1, D), x_sample.reshape(-1, D)], axis=0)
    T = x.shape[0]
    tab = _rope_tables(max(S for _, S in seqs))
    bf = lambda w: w.astype(BF16)
    w_in, w_branch_a, w_branch_b, w_out = bf(w_in), bf(w_branch_a), bf(w_branch_b), bf(w_out)
    ffn_w_gate, ffn_w_up, ffn_w_down = bf(ffn_w_gate), bf(ffn_w_up), bf(ffn_w_down)
    merge = lambda w: bf(w).reshape((-1,) + w.shape[2:])
    moe_w_gate, moe_w_up, moe_w_down = merge(moe_w_gate), merge(moe_w_up), merge(moe_w_down)

    xf, xb = _ln_in(x, ln_in_g, ln_in_b, tl['ln'])
    for l in range(depth):
        lam_init = 0.8 - 0.6 * math.exp(-0.3 * l)
        consts = jnp.full((1, LANES), lam_init, F32)
        proj = _proj(xb, w_in, l, tab, seqs, tl['proj'])

        oa = []
        for g in range(len(A_GROUPS)):
            prev, row0 = None, 0
            for n_seq, S in seqs:
                prev = _attn_a_group(proj, prev, g, n_seq, S, row0, T)
                row0 += n_seq * S
            oa.append(prev)
        ob, row0 = None, 0
        for n_seq, S in seqs:
            ob = _attn_b(proj, ob, diff_lambda[l], diff_subln_g[l], consts, n_seq, S, row0, T, tl['attn_b'])
            row0 += n_seq * S

        moe_layer = l % 2 == 1
        i = l // 2
        wr = None
        if moe_layer:
            wr = jnp.zeros((D, LANES), F32).at[:, :N_EXPERTS].set(moe_router[i])
        res = _post(oa, ob, proj, xf, w_branch_a, w_branch_b, w_out, l,
                    ln_mix_g[l], ln_mix_b[l], wr, alpha, tl['post'])
        if moe_layer:
            xf, xb, logits = res
            route = _route(logits, tl['moe_m'])
            y = _moe(xf, route, moe_w_gate, moe_w_up, moe_w_down, i, tl['moe_m'], tl['moe_f'])
            if l == depth - 1:
                n_p = seqs[0][0] * seqs[0][1]
                outs = [_combine(xf, y, ln_ffn_g[l], ln_ffn_b[l], alpha, tl['comb'], r0, n, False)[0]
                        for r0, n in ((0, n_p), (n_p, T - n_p))]
                return (outs[0].reshape(x_prompt.shape), outs[1].reshape(x_sample.shape))
            xf, xb = _combine(xf, y, ln_ffn_g[l], ln_ffn_b[l], alpha, tl['comb'], 0, T, True)
        else:
            xf, xb = res
            xf, xb = _ffn(xb, xf, ffn_w_gate, ffn_w_up, ffn_w_down, i,
                          ln_ffn_g[l], ln_ffn_b[l], alpha, tl['ffn_m'], tl['ffn_f'])

    n_p = seqs[0][0] * seqs[0][1]
    return (xf[:n_p].reshape(x_prompt.shape), xf[n_p:].reshape(x_sample.shape))
```

```python
import functools
import math

import jax
import jax.numpy as jnp
from jax import lax
from jax.experimental import pallas as pl
from jax.experimental.pallas import tpu as pltpu

F32 = jnp.float32
BF16 = jnp.bfloat16

HEAD_DIM = 128
A_GROUPS = ((128, 1), (512, 4), (2048, 16))
A_HEADS_PER_GROUP = 4
A_OUT = A_HEADS_PER_GROUP * HEAD_DIM
A_QKV = len(A_GROUPS) * A_OUT
B_HEADS = 4
B_QK_DIM = 64
B_V_DIM = 2 * B_QK_DIM
B_COLS = B_HEADS * B_V_DIM
ROPE_THETA = 500000.0
ROPE_FRACTION = 4
N_EXPERTS = 8
TOP_K = 2
LN_EPS = 1e-5
MASK_VALUE = -1e30
LOG2_E = math.log2(math.e)
GATE_COL0 = 3 * A_QKV + 3 * B_COLS

LANES = 128
COL_BLOCK = 512
VMEM_LIMIT = 56 * 1024 * 1024


def _cparams(sem):
    return pltpu.CompilerParams(dimension_semantics=sem, vmem_limit_bytes=VMEM_LIMIT)


def _layer_norm(y, g, b):
    mu = jnp.mean(y, axis=-1, keepdims=True)
    yc = y - mu
    var = jnp.mean(yc * yc, axis=-1, keepdims=True)
    return yc * lax.rsqrt(var + LN_EPS) * g + b


def _dot(a, b):
    return jnp.dot(a, b, preferred_element_type=F32)


def _dot_nt(a, b):
    return lax.dot_general(a, b, (((1,), (1,)), ((), ())), preferred_element_type=F32)


def _ln_in_kernel(x_ref, g_ref, b_ref, xf_ref, xb_ref):
    y = _layer_norm(x_ref[...], g_ref[...], b_ref[...])
    xf_ref[...] = y
    xb_ref[...] = y.astype(BF16)


def _ln_in(x, g, b, tm):
    T, D = x.shape
    row = pl.BlockSpec((tm, D), lambda i: (i, 0))
    vec = pl.BlockSpec((1, D), lambda i: (0, 0))
    return pl.pallas_call(
        _ln_in_kernel,
        grid=(T // tm,),
        in_specs=[row, vec, vec],
        out_specs=[row, row],
        out_shape=[jax.ShapeDtypeStruct((T, D), F32), jax.ShapeDtypeStruct((T, D), BF16)],
        compiler_params=_cparams(("parallel",)),
    )(x, g.reshape(1, D), b.reshape(1, D))


def _rope_tables(s_max):
    pos = jnp.arange(s_max, dtype=F32)[:, None]

    def tables(head, n_rep):
        rot = head // ROPE_FRACTION
        half = rot // 2
        inv_freq = ROPE_THETA ** (-jnp.arange(half, dtype=F32) / half)
        ang = pos * inv_freq[None, :]
        cos, sin = jnp.cos(ang), jnp.sin(ang)
        zero = jnp.zeros((s_max, head - rot), F32)
        zh = jnp.zeros((s_max, half), F32)
        c = jnp.concatenate([cos, cos, jnp.ones((s_max, head - rot), F32)], axis=1)
        s_up = jnp.concatenate([-sin, zh, zero], axis=1)
        s_dn = jnp.concatenate([zh, sin, zero], axis=1)
        return [jnp.tile(t, (1, n_rep)) for t in (c, s_up, s_dn)]

    return jnp.stack(tables(HEAD_DIM, 1) + tables(B_QK_DIM, 2), axis=0)


def _proj_kernel(x_ref, w_ref, tab_ref, o_ref):
    j = pl.program_id(1)
    n_a = A_QKV // COL_BLOCK
    is_a = j < 2 * n_a
    jb = j - 3 * n_a
    is_b = (jb == 0) | (jb == 1)

    def rope(t0, half, scale):
        n_parts = 4
        part = x_ref.shape[0] // n_parts
        parts = [pl.ds(a * part, part) for a in range(n_parts)]
        accs = [_dot(x_ref[rows, :], w_ref[...]) for rows in parts]
        for rows, acc in zip(parts, accs):
            for c in range(COL_BLOCK // LANES):
                sl = slice(c * LANES, (c + 1) * LANES)
                xc = acc[:, sl]
                up = pltpu.roll(xc, LANES - half, 1)
                dn = pltpu.roll(xc, half, 1)
                y = xc * tab_ref[t0, rows, :] + up * tab_ref[t0 + 1, rows, :] + dn * tab_ref[t0 + 2, rows, :]
                o_ref[rows, sl] = (y * scale).astype(BF16)

    @pl.when(is_a)
    def _():
        rope(0, HEAD_DIM // ROPE_FRACTION // 2, jnp.where(j < n_a, HEAD_DIM ** -0.5, 1.0).astype(F32))

    @pl.when(is_b)
    def _():
        rope(3, B_QK_DIM // ROPE_FRACTION // 2, jnp.where(jb == 0, B_QK_DIM ** -0.5 * LOG2_E, 1.0).astype(F32))

    @pl.when(jnp.logical_not(is_a | is_b))
    def _():
        o_ref[...] = _dot(x_ref[...], w_ref[...]).astype(BF16)


def _proj(xb, w, l, tab, seqs, tm):
    T, D = xb.shape
    C = w.shape[2]
    bounds = []
    t0 = 0
    for n_seq, S in seqs:
        assert S % tm == 0
        bounds.append((t0 // tm, S // tm))
        t0 += n_seq * S
    assert t0 == T

    def pos_block(i):
        blk = (i - bounds[0][0]) % bounds[0][1]
        for first, per in bounds[1:]:
            blk = jnp.where(i >= first, (i - first) % per, blk)
        return blk

    return pl.pallas_call(
        _proj_kernel,
        grid=(T // tm, C // COL_BLOCK),
        in_specs=[
            pl.BlockSpec((tm, D), lambda i, j: (i, 0)),
            pl.BlockSpec((None, D, COL_BLOCK), lambda i, j: (l, 0, j)),
            pl.BlockSpec((6, tm, LANES), lambda i, j: (0, pos_block(i), 0)),
        ],
        out_specs=pl.BlockSpec((tm, COL_BLOCK), lambda i, j: (i, j)),
        out_shape=jax.ShapeDtypeStruct((T, C), BF16),
        compiler_params=_cparams(("parallel", "arbitrary")),
    )(xb, w, tab)


A_TILES_IN_FLIGHT = 8


def _attend_tiles(tiles, n_side):
    scores = [_dot_nt(q, k) for q, k, _, _, _ in tiles]
    probs, stats = [], []
    for s, (_, _, _, q0, start) in zip(scores, tiles):
        tq, kw = s.shape
        qpos = q0 + lax.broadcasted_iota(jnp.int32, (tq, 1), 0)
        kpos = start + lax.broadcasted_iota(jnp.int32, (1, kw), 1)
        s = jnp.where(jnp.abs(kpos - qpos) <= n_side, s, MASK_VALUE)
        m = jnp.max(s, axis=-1, keepdims=True)
        p = jnp.exp(s - m)
        probs.append(p.astype(BF16))
        stats.append((m, jnp.sum(p, axis=-1, keepdims=True)))
    outs = [_dot(p, v) for p, (_, _, v, _, _) in zip(probs, tiles)]
    return [(o / den, m + jnp.log(den)) for o, (m, den) in zip(outs, stats)]


def _attn_a_kernel(q_ref, k_ref, v_ref, *rest, L, tq, kw, n_side):
    o_ref, lse_ref = rest[-2], rest[-1]
    n_q = L // tq
    group = max(1, A_TILES_IN_FLIGHT // A_HEADS_PER_GROUP)
    assert n_q % group == 0

    def body(it, carry):
        where, tiles = [], []
        for t in range(group):
            q0 = pl.multiple_of((it * group + t) * tq, tq)
            start = pl.multiple_of(jnp.clip(q0 - n_side, 0, L - kw), n_side)
            for h in range(A_HEADS_PER_GROUP):
                sl = slice(h * HEAD_DIM, (h + 1) * HEAD_DIM)
                where.append((q0, sl))
                tiles.append((q_ref[pl.ds(q0, tq), sl], k_ref[pl.ds(start, kw), sl],
                              v_ref[pl.ds(start, kw), sl], q0, start))
        for (q0, sl), (o, lse) in zip(where, _attend_tiles(tiles, n_side)):
            o_ref[pl.ds(q0, tq), sl] = o.astype(BF16)
            lse_ref[pl.ds(q0, tq), sl] = jnp.broadcast_to(lse, (tq, HEAD_DIM))
        return carry

    lax.fori_loop(0, n_q // group, body, 0)


def _attn_a_dil_kernel(q_ref, k_ref, v_ref, *rest, S, d, tq, kw, n_side, classes):
    o_ref, lse_ref, qf_ref, kf_ref, vf_ref = rest[-5:]
    L = S // d
    qf_ref[...] = q_ref[...].astype(F32)
    kf_ref[...] = k_ref[...].astype(F32)
    vf_ref[...] = v_ref[...].astype(F32)

    def rows(first, n, r):
        return pl.ds(r + first * d, n, stride=d)

    def body(it, carry):
        where, tiles = [], []
        for u in range(classes):
            r = it * classes + u
            for qi in range(L // tq):
                q0 = qi * tq
                start = min(max(q0 - n_side, 0), L - kw)
                where.append(rows(q0, tq, r))
                tiles.append((qf_ref[rows(q0, tq, r), :].astype(BF16), kf_ref[rows(start, kw, r), :].astype(BF16),
                              vf_ref[rows(start, kw, r), :].astype(BF16), q0, start))
        for idx, (o, lse) in zip(where, _attend_tiles(tiles, n_side)):
            o_ref[idx, :] = o
            lse_ref[idx, :] = jnp.broadcast_to(lse, (tq, HEAD_DIM))
        return carry

    lax.fori_loop(0, d // classes, body, 0)


def _attn_a_group(proj, prev, g, n_seq, S, row0, T):
    window, d = A_GROUPS[g]
    n_side = window // (2 * d)
    L = S // d
    tq = min(128, L)
    kw = min(L, tq + 2 * n_side)
    assert L % tq == 0 and tq % n_side == 0 and (L - kw) % n_side == 0
    n_g = len(A_GROUPS)
    blk0 = row0 // S
    o_dtype = BF16 if d == 1 else F32
    if prev is None:
        prev = (jnp.zeros((T, A_OUT), o_dtype), jnp.zeros((T, A_OUT), F32))
    hbm = pl.BlockSpec(memory_space=pl.ANY)
    out_shape = [jax.ShapeDtypeStruct((T, A_OUT), o_dtype), jax.ShapeDtypeStruct((T, A_OUT), F32)]
    if d == 1:
        spec = lambda part: pl.BlockSpec((S, COL_BLOCK), lambda b: (blk0 + b, part * n_g + g))
        out_spec = pl.BlockSpec((S, A_OUT), lambda b: (blk0 + b, 0))
        return pl.pallas_call(
            functools.partial(_attn_a_kernel, L=L, tq=tq, kw=kw, n_side=n_side),
            grid=(n_seq,),
            in_specs=[spec(0), spec(1), spec(2), hbm, hbm],
            out_specs=[out_spec, out_spec],
            out_shape=out_shape,
            input_output_aliases={3: 0, 4: 1},
            compiler_params=_cparams(("parallel",)),
        )(proj, proj, proj, *prev)
    heads = n_g * A_HEADS_PER_GROUP
    spec = lambda part: pl.BlockSpec((S, HEAD_DIM), lambda b, h: (blk0 + b, part * heads + g * A_HEADS_PER_GROUP + h))
    out_spec = pl.BlockSpec((S, HEAD_DIM), lambda b, h: (blk0 + b, h))
    classes = max(1, min(d, A_TILES_IN_FLIGHT // (L // tq)))
    assert d % classes == 0
    return pl.pallas_call(
        functools.partial(_attn_a_dil_kernel, S=S, d=d, tq=tq, kw=kw, n_side=n_side, classes=classes),
        grid=(n_seq, A_HEADS_PER_GROUP),
        in_specs=[spec(0), spec(1), spec(2), hbm, hbm],
        out_specs=[out_spec, out_spec],
        out_shape=out_shape,
        input_output_aliases={3: 0, 4: 1},
        scratch_shapes=[pltpu.VMEM((S, HEAD_DIM), F32)] * 3,
        compiler_params=_cparams(("parallel", "parallel")),
    )(proj, proj, proj, *prev)


def _attn_b_kernel(q_ref, k_ref, v_ref, lam_ref, g_ref, c_ref, *rest):
    o_ref = rest[-1]
    k = k_ref[...]
    v = v_ref[...]
    lane = lax.broadcasted_iota(jnp.int32, (1, 2 * B_QK_DIM), 1)
    lam_init = c_ref[0:1, 0:1]
    lv = lam_ref[...]
    lam = (jnp.exp(jnp.sum(lv[0:1] * lv[1:2], axis=-1, keepdims=True))
           - jnp.exp(jnp.sum(lv[2:3] * lv[3:4], axis=-1, keepdims=True)) + lam_init)

    half = q_ref.shape[0] // 2
    halves = (pl.ds(0, half), pl.ds(half, half))
    scores = []
    for rows in halves:
        q = q_ref[rows, :]
        zero = jnp.zeros_like(q)
        scores.append((_dot_nt(jnp.where(lane < B_QK_DIM, q, zero), k),
                       _dot_nt(jnp.where(lane >= B_QK_DIM, q, zero), k)))

    def softmax(s):
        p = jnp.exp2(s - jnp.max(s, axis=-1, keepdims=True))
        return p, 1.0 / jnp.sum(p, axis=-1, keepdims=True)

    for rows, (s1, s2) in zip(halves, scores):
        p1, r1 = softmax(s1)
        p2, r2 = softmax(s2)
        o = _dot((p1 * r1 - p2 * (lam * r2)).astype(BF16), v)
        o = o * lax.rsqrt(jnp.mean(o * o, axis=-1, keepdims=True) + LN_EPS) * g_ref[...] * (1.0 - lam_init)
        o_ref[rows, :] = o.astype(BF16)


def _attn_b(proj, prev, lam_vecs, subln_g, consts, n_seq, S, row0, T, tq):
    C = proj.shape[1]
    blk0 = row0 // S
    qb0 = row0 // tq
    n_qt = S // tq
    col = lambda part: (3 * A_QKV + part * B_COLS) // LANES
    in_specs = [
        pl.BlockSpec((tq, LANES), lambda b, h, i: (qb0 + b * n_qt + i, col(0) + h)),
        pl.BlockSpec((S, LANES), lambda b, h, i: (blk0 + b, col(1) + h)),
        pl.BlockSpec((S, LANES), lambda b, h, i: (blk0 + b, col(2) + h)),
        pl.BlockSpec((4, B_QK_DIM), lambda b, h, i: (0, 0)),
        pl.BlockSpec((1, B_V_DIM), lambda b, h, i: (0, 0)),
        pl.BlockSpec((1, LANES), lambda b, h, i: (0, 0)),
    ]
    if prev is None:
        prev = jnp.zeros((T, B_COLS), BF16)
    in_specs.append(pl.BlockSpec(memory_space=pl.ANY))
    args = [proj, proj, proj, lam_vecs, subln_g.reshape(1, B_V_DIM), consts, prev]
    aliases = {6: 0}
    return pl.pallas_call(
        _attn_b_kernel,
        grid=(n_seq, B_HEADS, n_qt),
        in_specs=in_specs,
        out_specs=pl.BlockSpec((tq, LANES), lambda b, h, i: (qb0 + b * n_qt + i, h)),
        out_shape=jax.ShapeDtypeStruct((T, B_COLS), BF16),
        input_output_aliases=aliases,
        compiler_params=_cparams(("parallel", "parallel", "arbitrary")),
    )(*args)


def _post_kernel(o1_ref, o2_ref, o3_ref, l1_ref, l2_ref, l3_ref, ob_ref, ga_ref, gb_ref, x_ref,
                 wa_ref, wb_ref, wo_ref, g_ref, b_ref, *rest, alpha, with_router):
    if with_router:
        wrh_ref, wrl_ref, xf_ref, xb_ref, lg_ref = rest
    else:
        xf_ref, xb_ref = rest
    half = x_ref.shape[0] // 2
    halves = (pl.ds(0, half), pl.ds(half, half))
    branches = []
    for rows in halves:
        l1, l2, l3 = l1_ref[rows, :], l2_ref[rows, :], l3_ref[rows, :]
        m = jnp.maximum(jnp.maximum(l1, l2), l3)
        e1, e2, e3 = jnp.exp(l1 - m), jnp.exp(l2 - m), jnp.exp(l3 - m)
        oa = (e1 * o1_ref[rows, :].astype(F32) + e2 * o2_ref[rows, :].astype(F32)
              + e3 * o3_ref[rows, :].astype(F32)) / (e1 + e2 + e3)
        branches.append((_dot(oa.astype(BF16), wa_ref[...]), _dot(ob_ref[rows, :], wb_ref[...])))
    zs = []
    for rows, (ya, yb) in zip(halves, branches):
        mix = (jax.nn.sigmoid(ga_ref[rows, :].astype(F32)) * ya
               + jax.nn.sigmoid(gb_ref[rows, :].astype(F32)) * yb)
        zs.append(_dot(mix.astype(BF16), wo_ref[...]))
    for rows, z in zip(halves, zs):
        y = _layer_norm(alpha * x_ref[rows, :] + z, g_ref[...], b_ref[...])
        yh = y.astype(BF16)
        xf_ref[rows, :] = y
        xb_ref[rows, :] = yh
        if with_router:
            yl = (y - yh.astype(F32)).astype(BF16)
            lg_ref[rows, :] = _dot(yh, wrh_ref[...]) + (_dot(yl, wrh_ref[...]) + _dot(yh, wrl_ref[...]))


def _post(oa, ob, proj, x, wa, wb, wo, l, g, b, wr, alpha, tm):
    T, D = x.shape
    assert GATE_COL0 % D == 0
    gcol = GATE_COL0 // D
    row = lambda w: pl.BlockSpec((tm, w), lambda i: (i, 0))
    full = lambda a: pl.BlockSpec(a.shape, lambda i: (0,) * a.ndim)
    layer = lambda a: pl.BlockSpec((None,) + a.shape[1:], lambda i: (l, 0, 0))
    g2, b2 = g.reshape(1, D), b.reshape(1, D)
    (o1, l1), (o2, l2), (o3, l3) = oa
    in_specs = [row(A_OUT)] * 6 + [row(B_COLS),
                                   pl.BlockSpec((tm, D), lambda i: (i, gcol)),
                                   pl.BlockSpec((tm, D), lambda i: (i, gcol + 1)),
                                   row(D), layer(wa), layer(wb), layer(wo), full(g2), full(b2)]
    args = [o1, o2, o3, l1, l2, l3, ob, proj, proj, x, wa, wb, wo, g2, b2]
    out_specs = [row(D), row(D)]
    out_shape = [jax.ShapeDtypeStruct((T, D), F32), jax.ShapeDtypeStruct((T, D), BF16)]
    if wr is not None:
        wr_hi = wr.astype(BF16)
        wr_lo = (wr - wr_hi.astype(F32)).astype(BF16)
        in_specs += [full(wr_hi), full(wr_lo)]
        args += [wr_hi, wr_lo]
        out_specs.append(row(LANES))
        out_shape.append(jax.ShapeDtypeStruct((T, LANES), F32))
    return pl.pallas_call(
        functools.partial(_post_kernel, alpha=alpha, with_router=wr is not None),
        grid=(T // tm,),
        in_specs=in_specs,
        out_specs=out_specs,
        out_shape=out_shape,
        compiler_params=_cparams(("parallel",)),
    )(*args)


def _ffn_kernel(xb_ref, x_ref, wg_ref, wu_ref, wd_ref, g_ref, b_ref, xf_ref, xo_ref, acc_ref, *, alpha):
    f = pl.program_id(1)

    @pl.when(f == 0)
    def _():
        acc_ref[...] = jnp.zeros_like(acc_ref)

    xb = xb_ref[...]
    h = jax.nn.silu(_dot(xb, wg_ref[...])) * _dot(xb, wu_ref[...])
    acc_ref[...] += _dot(h.astype(BF16), wd_ref[...])

    @pl.when(f == pl.num_programs(1) - 1)
    def _():
        y = _layer_norm(alpha * x_ref[...] + acc_ref[...], g_ref[...], b_ref[...])
        xf_ref[...] = y
        xo_ref[...] = y.astype(BF16)


def _ffn(xb, x, wg, wu, wd, li, g, b, alpha, tm, tf):
    T, D = x.shape
    F = wg.shape[2]
    row = pl.BlockSpec((tm, D), lambda i, f: (i, 0))
    vec = pl.BlockSpec((1, D), lambda i, f: (0, 0))
    return pl.pallas_call(
        functools.partial(_ffn_kernel, alpha=alpha),
        grid=(T // tm, F // tf),
        in_specs=[row, row,
                  pl.BlockSpec((None, D, tf), lambda i, f: (li, 0, f)),
                  pl.BlockSpec((None, D, tf), lambda i, f: (li, 0, f)),
                  pl.BlockSpec((None, tf, D), lambda i, f: (li, f, 0)),
                  vec, vec],
        out_specs=[row, row],
        out_shape=[jax.ShapeDtypeStruct((T, D), F32), jax.ShapeDtypeStruct((T, D), BF16)],
        scratch_shapes=[pltpu.VMEM((tm, D), F32)],
        compiler_params=_cparams(("parallel", "arbitrary")),
    )(xb, x, wg, wu, wd, g.reshape(1, D), b.reshape(1, D))


def _route(logits, tm):
    T = logits.shape[0]
    top_logits, top_idx = lax.top_k(logits[:, :N_EXPERTS], TOP_K)
    gates = jax.nn.softmax(top_logits, axis=-1)
    n_assign = T * TOP_K
    flat_e = top_idx.reshape(n_assign).astype(jnp.int32)
    onehot = (flat_e[:, None] == jnp.arange(N_EXPERTS, dtype=jnp.int32)[None, :]).astype(jnp.int32)
    csum = jnp.cumsum(onehot, axis=0)
    counts = csum[-1]
    rank = jnp.sum((csum - onehot) * onehot, axis=1)
    padded = (counts + tm - 1) // tm * tm
    pad_end = jnp.cumsum(padded)
    pad_start = pad_end - padded
    dest = pad_start[flat_e] + rank
    n_blocks = -(-(n_assign + N_EXPERTS * (tm - 1)) // tm)
    n_rows = n_blocks * tm
    row_assign = jnp.full((n_rows,), -1, jnp.int32).at[dest].set(jnp.arange(n_assign, dtype=jnp.int32))
    used = row_assign >= 0
    row_code = jnp.where(used, (row_assign // TOP_K) * 4 + (row_assign % TOP_K) * 2 + 1, 0)
    row_gate = jnp.where(used, gates.reshape(n_assign)[jnp.maximum(row_assign, 0)], 0.0)
    block_start = jnp.arange(n_blocks, dtype=jnp.int32) * tm
    block_expert = jnp.minimum(jnp.searchsorted(pad_end, block_start, side='right'), N_EXPERTS - 1).astype(jnp.int32)
    n_valid = (pad_end[-1] // tm).astype(jnp.int32).reshape(1)
    return block_expert, n_valid, row_code, jnp.broadcast_to(row_gate[:, None], (n_rows, LANES))


def _moe_kernel(be_ref, nv_ref, code_ref, x_hbm, gate_ref, wg_ref, wu_ref, wd_ref, y_hbm,
                xg_ref, xb_ref, acc_ref, yo_ref, sem_in, sem_out, *, tm, n_tok, chunk):
    i = pl.program_id(0)
    f = pl.program_id(1)
    n_f = pl.num_programs(1)
    n_blk = pl.num_programs(0)
    n_live = nv_ref[0]
    live = i < n_live
    slot = i % 2
    n_chunks = tm // chunk

    def gather_chunk(blk, buf, c):
        r0 = pl.multiple_of(c * chunk, chunk)
        for j in range(chunk):
            tok = code_ref[blk * tm + r0 + j] >> 2
            pltpu.make_async_copy(x_hbm.at[pl.ds(tok, 1)], xg_ref.at[buf, pl.ds(r0 + j, 1)], sem_in.at[buf]).start()

    def scatter_chunk(blk, c):
        r0 = pl.multiple_of(c * chunk, chunk)
        for j in range(chunk):
            code = code_ref[blk * tm + r0 + j]
            dst = jnp.where((code & 1) == 1, ((code >> 1) & 1) * n_tok + (code >> 2), TOP_K * n_tok + r0 + j)
            pltpu.make_async_copy(yo_ref.at[pl.ds(r0 + j, 1)], y_hbm.at[pl.ds(dst, 1)], sem_out).start()

    def wait_scatter():
        pltpu.make_async_copy(yo_ref, yo_ref, sem_out).wait()

    @pl.when((i == 0) & (f == 0))
    def _():
        def body(c, carry):
            gather_chunk(0, 0, c)
            return carry

        lax.fori_loop(0, n_chunks, body, 0)
        yo_ref[...] = jnp.zeros_like(yo_ref)
        spare = pltpu.make_async_copy(yo_ref, y_hbm.at[pl.ds(TOP_K * n_tok, tm)], sem_out)
        spare.start()
        spare.wait()

    @pl.when(live & (f == 0))
    def _():
        pltpu.make_async_copy(xg_ref.at[slot], xg_ref.at[slot], sem_in.at[slot]).wait()
        xb_ref[...] = xg_ref[slot].astype(BF16)
        acc_ref[...] = jnp.zeros_like(acc_ref)

    @pl.when((f < n_chunks) & (i + 1 < n_live))
    def _():
        gather_chunk(i + 1, 1 - slot, f)

    @pl.when((f < n_chunks) & (i >= 1) & (i <= n_live))
    def _():
        scatter_chunk(i - 1, f)

    @pl.when(live)
    def _():
        xb = xb_ref[...]
        h = jax.nn.silu(_dot(xb, wg_ref[0])) * _dot(xb, wu_ref[0])
        acc_ref[...] += _dot(h.astype(BF16), wd_ref[0])

    @pl.when((f == n_f - 1) & (i >= 1) & (i <= n_live))
    def _():
        wait_scatter()

    @pl.when(live & (f == n_f - 1))
    def _():
        gate = gate_ref[...]
        for c in range(acc_ref.shape[1] // LANES):
            sl = slice(c * LANES, (c + 1) * LANES)
            yo_ref[:, sl] = acc_ref[:, sl] * gate

        @pl.when(i == n_blk - 1)
        def _():
            def body(c, carry):
                scatter_chunk(i, c)
                return carry

            lax.fori_loop(0, n_chunks, body, 0)
            wait_scatter()


def _moe(x, route, wg, wu, wd, li, tm, tf):
    T, D = x.shape
    F = wg.shape[2]
    nf = F // tf
    block_expert, n_valid, row_code, row_gate = route
    n_blocks = block_expert.shape[0]
    chunk = next(c for c in range(8, tm + 1, 8) if tm % c == 0 and tm // c <= nf)

    def fidx(i, f, nv):
        return jnp.where(i < nv[0], f, nf - 1)

    grid_spec = pltpu.PrefetchScalarGridSpec(
        num_scalar_prefetch=3,
        grid=(n_blocks, nf),
        in_specs=[
            pl.BlockSpec(memory_space=pl.ANY),
            pl.BlockSpec((tm, LANES), lambda i, f, be, nv, rc: (i, 0)),
            pl.BlockSpec((1, D, tf), lambda i, f, be, nv, rc: (li * N_EXPERTS + be[i], 0, fidx(i, f, nv))),
            pl.BlockSpec((1, D, tf), lambda i, f, be, nv, rc: (li * N_EXPERTS + be[i], 0, fidx(i, f, nv))),
            pl.BlockSpec((1, tf, D), lambda i, f, be, nv, rc: (li * N_EXPERTS + be[i], fidx(i, f, nv), 0)),
        ],
        out_specs=pl.BlockSpec(memory_space=pl.ANY),
        scratch_shapes=[pltpu.VMEM((2, tm, D), F32), pltpu.VMEM((tm, D), BF16), pltpu.VMEM((tm, D), F32),
                        pltpu.VMEM((tm, D), F32), pltpu.SemaphoreType.DMA((2,)), pltpu.SemaphoreType.DMA(())],
    )
    y = pl.pallas_call(
        functools.partial(_moe_kernel, tm=tm, n_tok=T, chunk=chunk),
        grid_spec=grid_spec,
        out_shape=jax.ShapeDtypeStruct((TOP_K * T + tm, D), F32),
        compiler_params=_cparams(("arbitrary", "arbitrary")),
    )(block_expert, n_valid, row_code, x, row_gate, wg, wu, wd)
    return y


def _combine_kernel(x_ref, y0_ref, y1_ref, g_ref, b_ref, xf_ref, *rest, alpha):
    y = _layer_norm(alpha * x_ref[...] + (y0_ref[...] + y1_ref[...]), g_ref[...], b_ref[...])
    xf_ref[...] = y
    for xb_ref in rest:
        xb_ref[...] = y.astype(BF16)


def _combine(x, y, g, b, alpha, tm, row0, n_rows, with_bf16):
    T, D = x.shape
    n = T // tm
    i0 = row0 // tm
    row = pl.BlockSpec((tm, D), lambda i: (i, 0))
    vec = pl.BlockSpec((1, D), lambda i: (0, 0))
    out_specs = [row, row] if with_bf16 else [row]
    out_shape = [jax.ShapeDtypeStruct((n_rows, D), F32)]
    if with_bf16:
        out_shape.append(jax.ShapeDtypeStruct((n_rows, D), BF16))
    return pl.pallas_call(
        functools.partial(_combine_kernel, alpha=alpha),
        grid=(n_rows // tm,),
        in_specs=[pl.BlockSpec((tm, D), lambda i: (i0 + i, 0)),
                  pl.BlockSpec((tm, D), lambda i: (i0 + i, 0)),
                  pl.BlockSpec((tm, D), lambda i: (n + i0 + i, 0)), vec, vec],
        out_specs=out_specs,
        out_shape=out_shape,
        compiler_params=_cparams(("parallel",)),
    )(x, y, y, g.reshape(1, D), b.reshape(1, D))


def _tiles(D):
    big = D >= 2048
    return dict(ln=512, proj=2048, attn_b=512, post=256 if big else 512, ffn_m=512, ffn_f=512,
                moe_m=512, moe_f=1024, comb=512)


def kernel(x_prompt, x_sample, ln_in_g, ln_in_b, w_in, w_branch_a, w_branch_b, w_out, diff_lambda, diff_subln_g, ln_mix_g, ln_mix_b, ln_ffn_g, ln_ffn_b, ffn_w_gate, ffn_w_up, ffn_w_down, moe_router, moe_w_gate, moe_w_up, moe_w_down):
    depth = w_in.shape[0]
    D = x_prompt.shape[-1]
    alpha = (2 * depth) ** 0.25
    tl = _tiles(D)
    seqs = [(x_prompt.shape[0], x_prompt.shape[1]), (x_sample.shape[0], x_sample.shape[1])]
    x = jnp.concatenate([x_prompt.reshape(-1, D), x_sample.reshape(-1, D)], axis=0)
    T = x.shape[0]
    tab = _rope_tables(max(S for _, S in seqs))
    bf = lambda w: w.astype(BF16)
    w_in, w_branch_a, w_branch_b, w_out = bf(w_in), bf(w_branch_a), bf(w_branch_b), bf(w_out)
    ffn_w_gate, ffn_w_up, ffn_w_down = bf(ffn_w_gate), bf(ffn_w_up), bf(ffn_w_down)
    merge = lambda w: bf(w).reshape((-1,) + w.shape[2:])
    moe_w_gate, moe_w_up, moe_w_down = merge(moe_w_gate), merge(moe_w_up), merge(moe_w_down)

    xf, xb = _ln_in(x, ln_in_g, ln_in_b, tl['ln'])
    for l in range(depth):
        lam_init = 0.8 - 0.6 * math.exp(-0.3 * l)
        consts = jnp.full((1, LANES), lam_init, F32)
        proj = _proj(xb, w_in, l, tab, seqs, tl['proj'])

        oa = []
        for g in range(len(A_GROUPS)):
            prev, row0 = None, 0
            for n_seq, S in seqs:
                prev = _attn_a_group(proj, prev, g, n_seq, S, row0, T)
                row0 += n_seq * S
            oa.append(prev)
        ob, row0 = None, 0
        for n_seq, S in seqs:
            ob = _attn_b(proj, ob, diff_lambda[l], diff_subln_g[l], consts, n_seq, S, row0, T, tl['attn_b'])
            row0 += n_seq * S

        moe_layer = l % 2 == 1
        i = l // 2
        wr = None
        if moe_layer:
            wr = jnp.zeros((D, LANES), F32).at[:, :N_EXPERTS].set(moe_router[i])
        res = _post(oa, ob, proj, xf, w_branch_a, w_branch_b, w_out, l,
                    ln_mix_g[l], ln_mix_b[l], wr, alpha, tl['post'])
        if moe_layer:
            xf, xb, logits = res
            route = _route(logits, tl['moe_m'])
            y = _moe(xf, route, moe_w_gate, moe_w_up, moe_w_down, i, tl['moe_m'], tl['moe_f'])
            if l == depth - 1:
                n_p = seqs[0][0] * seqs[0][1]
                outs = [_combine(xf, y, ln_ffn_g[l], ln_ffn_b[l], alpha, tl['comb'], r0, n, False)[0]
                        for r0, n in ((0, n_p), (n_p, T - n_p))]
                return (outs[0].reshape(x_prompt.shape), outs[1].reshape(x_sample.shape))
            xf, xb = _combine(xf, y, ln_ffn_g[l], ln_ffn_b[l], alpha, tl['comb'], 0, T, True)
        else:
            xf, xb = res
            xf, xb = _ffn(xb, xf, ffn_w_gate, ffn_w_up, ffn_w_down, i,
                          ln_ffn_g[l], ln_ffn_b[l], alpha, tl['ffn_m'], tl['ffn_f'])

    n_p = seqs[0][0] * seqs[0][1]
    return (xf[:n_p].reshape(x_prompt.shape), xf[n_p:].reshape(x_sample.shape))
```

```python
import functools
import math

import jax
import jax.numpy as jnp
from jax import lax
from jax.experimental import pallas as pl
from jax.experimental.pallas import tpu as pltpu

F32 = jnp.float32
BF16 = jnp.bfloat16

HEAD_DIM = 128
A_GROUPS = ((128, 1), (512, 4), (2048, 16))
A_HEADS_PER_GROUP = 4
A_OUT = A_HEADS_PER_GROUP * HEAD_DIM
A_QKV = len(A_GROUPS) * A_OUT
B_HEADS = 4
B_QK_DIM = 64
B_V_DIM = 2 * B_QK_DIM
B_COLS = B_HEADS * B_V_DIM
ROPE_THETA = 500000.0
ROPE_FRACTION = 4
N_EXPERTS = 8
TOP_K = 2
LN_EPS = 1e-5
MASK_VALUE = -1e30
LOG2_E = math.log2(math.e)
GATE_COL0 = 3 * A_QKV + 3 * B_COLS

LANES = 128
COL_BLOCK = 512
VMEM_LIMIT = 56 * 1024 * 1024


def _cparams(sem):
    return pltpu.CompilerParams(dimension_semantics=sem, vmem_limit_bytes=VMEM_LIMIT)


def _layer_norm(y, g, b):
    mu = jnp.mean(y, axis=-1, keepdims=True)
    yc = y - mu
    var = jnp.mean(yc * yc, axis=-1, keepdims=True)
    return yc * lax.rsqrt(var + LN_EPS) * g + b


def _dot(a, b):
    return jnp.dot(a, b, preferred_element_type=F32)


def _dot_nt(a, b):
    return lax.dot_general(a, b, (((1,), (1,)), ((), ())), preferred_element_type=F32)


def _ln_in_kernel(x_ref, g_ref, b_ref, xf_ref, xb_ref):
    y = _layer_norm(x_ref[...], g_ref[...], b_ref[...])
    xf_ref[...] = y
    xb_ref[...] = y.astype(BF16)


def _ln_in(x, g, b, tm):
    T, D = x.shape
    row = pl.BlockSpec((tm, D), lambda i: (i, 0))
    vec = pl.BlockSpec((1, D), lambda i: (0, 0))
    return pl.pallas_call(
        _ln_in_kernel,
        grid=(T // tm,),
        in_specs=[row, vec, vec],
        out_specs=[row, row],
        out_shape=[jax.ShapeDtypeStruct((T, D), F32), jax.ShapeDtypeStruct((T, D), BF16)],
        compiler_params=_cparams(("parallel",)),
    )(x, g.reshape(1, D), b.reshape(1, D))


def _rope_tables(s_max):
    pos = jnp.arange(s_max, dtype=F32)[:, None]

    def tables(head, n_rep):
        rot = head // ROPE_FRACTION
        half = rot // 2
        inv_freq = ROPE_THETA ** (-jnp.arange(half, dtype=F32) / half)
        ang = pos * inv_freq[None, :]
        cos, sin = jnp.cos(ang), jnp.sin(ang)
        zero = jnp.zeros((s_max, head - rot), F32)
        zh = jnp.zeros((s_max, half), F32)
        c = jnp.concatenate([cos, cos, jnp.ones((s_max, head - rot), F32)], axis=1)
        s_up = jnp.concatenate([-sin, zh, zero], axis=1)
        s_dn = jnp.concatenate([zh, sin, zero], axis=1)
        return [jnp.tile(t, (1, n_rep)) for t in (c, s_up, s_dn)]

    return jnp.stack(tables(HEAD_DIM, 1) + tables(B_QK_DIM, 2), axis=0)


def _proj_kernel(x_ref, w_ref, tab_ref, o_ref):
    j = pl.program_id(1)
    n_a = A_QKV // COL_BLOCK
    is_a = j < 2 * n_a
    jb = j - 3 * n_a
    is_b = (jb == 0) | (jb == 1)

    def rope(t0, half, scale):
        n_parts = 4
        part = x_ref.shape[0] // n_parts
        parts = [pl.ds(a * part, part) for a in range(n_parts)]
        accs = [_dot(x_ref[rows, :], w_ref[...]) for rows in parts]
        for rows, acc in zip(parts, accs):
            for c in range(COL_BLOCK // LANES):
                sl = slice(c * LANES, (c + 1) * LANES)
                xc = acc[:, sl]
                up = pltpu.roll(xc, LANES - half, 1)
                dn = pltpu.roll(xc, half, 1)
                y = xc * tab_ref[t0, rows, :] + up * tab_ref[t0 + 1, rows, :] + dn * tab_ref[t0 + 2, rows, :]
                o_ref[rows, sl] = (y * scale).astype(BF16)

    @pl.when(is_a)
    def _():
        rope(0, HEAD_DIM // ROPE_FRACTION // 2, jnp.where(j < n_a, HEAD_DIM ** -0.5, 1.0).astype(F32))

    @pl.when(is_b)
    def _():
        rope(3, B_QK_DIM // ROPE_FRACTION // 2, jnp.where(jb == 0, B_QK_DIM ** -0.5 * LOG2_E, 1.0).astype(F32))

    @pl.when(jnp.logical_not(is_a | is_b))
    def _():
        o_ref[...] = _dot(x_ref[...], w_ref[...]).astype(BF16)


def _proj(xb, w, l, tab, seqs, tm):
    T, D = xb.shape
    C = w.shape[2]
    bounds = []
    t0 = 0
    for n_seq, S in seqs:
        assert S % tm == 0
        bounds.append((t0 // tm, S // tm))
        t0 += n_seq * S
    assert t0 == T

    def pos_block(i):
        blk = (i - bounds[0][0]) % bounds[0][1]
        for first, per in bounds[1:]:
            blk = jnp.where(i >= first, (i - first) % per, blk)
        return blk

    return pl.pallas_call(
        _proj_kernel,
        grid=(T // tm, C // COL_BLOCK),
        in_specs=[
            pl.BlockSpec((tm, D), lambda i, j: (i, 0)),
            pl.BlockSpec((None, D, COL_BLOCK), lambda i, j: (l, 0, j)),
            pl.BlockSpec((6, tm, LANES), lambda i, j: (0, pos_block(i), 0)),
        ],
        out_specs=pl.BlockSpec((tm, COL_BLOCK), lambda i, j: (i, j)),
        out_shape=jax.ShapeDtypeStruct((T, C), BF16),
        compiler_params=_cparams(("parallel", "arbitrary")),
    )(xb, w, tab)


A_TILES_IN_FLIGHT = 8


def _attend_tiles(tiles, n_side):
    scores = [_dot_nt(q, k) for q, k, _, _, _ in tiles]
    probs, stats = [], []
    for s, (_, _, _, q0, start) in zip(scores, tiles):
        tq, kw = s.shape
        qpos = q0 + lax.broadcasted_iota(jnp.int32, (tq, 1), 0)
        kpos = start + lax.broadcasted_iota(jnp.int32, (1, kw), 1)
        s = jnp.where(jnp.abs(kpos - qpos) <= n_side, s, MASK_VALUE)
        m = jnp.max(s, axis=-1, keepdims=True)
        p = jnp.exp(s - m)
        probs.append(p.astype(BF16))
        stats.append((m, jnp.sum(p, axis=-1, keepdims=True)))
    outs = [_dot(p, v) for p, (_, _, v, _, _) in zip(probs, tiles)]
    return [(o / den, m + jnp.log(den)) for o, (m, den) in zip(outs, stats)]


def _attn_a_kernel(q_ref, k_ref, v_ref, o_ref, lse_ref, *, L, tq, kw, n_side):
    n_q = L // tq
    group = max(1, A_TILES_IN_FLIGHT // A_HEADS_PER_GROUP)
    assert n_q % group == 0

    def body(it, carry):
        where, tiles = [], []
        for t in range(group):
            q0 = pl.multiple_of((it * group + t) * tq, tq)
            start = pl.multiple_of(jnp.clip(q0 - n_side, 0, L - kw), n_side)
            for h in range(A_HEADS_PER_GROUP):
                sl = slice(h * HEAD_DIM, (h + 1) * HEAD_DIM)
                where.append((q0, sl))
                tiles.append((q_ref[pl.ds(q0, tq), sl], k_ref[pl.ds(start, kw), sl],
                              v_ref[pl.ds(start, kw), sl], q0, start))
        for (q0, sl), (o, lse) in zip(where, _attend_tiles(tiles, n_side)):
            o_ref[pl.ds(q0, tq), sl] = o.astype(BF16)
            lse_ref[pl.ds(q0, tq), sl] = jnp.broadcast_to(lse, (tq, HEAD_DIM))
        return carry

    lax.fori_loop(0, n_q // group, body, 0)


def _attn_a_dil_kernel(q_ref, k_ref, v_ref, o_ref, lse_ref, qf_ref, kf_ref, vf_ref, *, S, d, tq, kw, n_side, classes):
    L = S // d
    qf_ref[...] = q_ref[...].astype(F32)
    kf_ref[...] = k_ref[...].astype(F32)
    vf_ref[...] = v_ref[...].astype(F32)

    def rows(first, n, r):
        return pl.ds(r + first * d, n, stride=d)

    def body(it, carry):
        where, tiles = [], []
        for u in range(classes):
            r = it * classes + u
            for qi in range(L // tq):
                q0 = qi * tq
                start = min(max(q0 - n_side, 0), L - kw)
                where.append(rows(q0, tq, r))
                tiles.append((qf_ref[rows(q0, tq, r), :].astype(BF16), kf_ref[rows(start, kw, r), :].astype(BF16),
                              vf_ref[rows(start, kw, r), :].astype(BF16), q0, start))
        for idx, (o, lse) in zip(where, _attend_tiles(tiles, n_side)):
            o_ref[idx, :] = o
            lse_ref[idx, :] = jnp.broadcast_to(lse, (tq, HEAD_DIM))
        return carry

    lax.fori_loop(0, d // classes, body, 0)


def _attn_a_group(proj, g, n_seq, S, row0):
    window, d = A_GROUPS[g]
    n_side = window // (2 * d)
    L = S // d
    tq = min(128, L)
    kw = min(L, tq + 2 * n_side)
    assert L % tq == 0 and tq % n_side == 0 and (L - kw) % n_side == 0
    n_g = len(A_GROUPS)
    blk0 = row0 // S
    o_dtype = BF16 if d == 1 else F32
    out_shape = [jax.ShapeDtypeStruct((n_seq * S, A_OUT), o_dtype), jax.ShapeDtypeStruct((n_seq * S, A_OUT), F32)]
    if d == 1:
        spec = lambda part: pl.BlockSpec((S, COL_BLOCK), lambda b: (blk0 + b, part * n_g + g))
        out_spec = pl.BlockSpec((S, A_OUT), lambda b: (b, 0))
        return pl.pallas_call(
            functools.partial(_attn_a_kernel, L=L, tq=tq, kw=kw, n_side=n_side),
            grid=(n_seq,),
            in_specs=[spec(0), spec(1), spec(2)],
            out_specs=[out_spec, out_spec],
            out_shape=out_shape,
            compiler_params=_cparams(("parallel",)),
        )(proj, proj, proj)
    heads = n_g * A_HEADS_PER_GROUP
    spec = lambda part: pl.BlockSpec((S, HEAD_DIM), lambda b, h: (blk0 + b, part * heads + g * A_HEADS_PER_GROUP + h))
    out_spec = pl.BlockSpec((S, HEAD_DIM), lambda b, h: (b, h))
    classes = max(1, min(d, A_TILES_IN_FLIGHT // (L // tq)))
    assert d % classes == 0
    return pl.pallas_call(
        functools.partial(_attn_a_dil_kernel, S=S, d=d, tq=tq, kw=kw, n_side=n_side, classes=classes),
        grid=(n_seq, A_HEADS_PER_GROUP),
        in_specs=[spec(0), spec(1), spec(2)],
        out_specs=[out_spec, out_spec],
        out_shape=out_shape,
        scratch_shapes=[pltpu.VMEM((S, HEAD_DIM), F32)] * 3,
        compiler_params=_cparams(("parallel", "parallel")),
    )(proj, proj, proj)


def _attn_b_kernel(q_ref, k_ref, v_ref, lam_ref, g_ref, c_ref, o_ref):
    k = k_ref[...]
    v = v_ref[...]
    lane = lax.broadcasted_iota(jnp.int32, (1, 2 * B_QK_DIM), 1)
    lam_init = c_ref[0:1, 0:1]
    lv = lam_ref[...]
    lam = (jnp.exp(jnp.sum(lv[0:1] * lv[1:2], axis=-1, keepdims=True))
           - jnp.exp(jnp.sum(lv[2:3] * lv[3:4], axis=-1, keepdims=True)) + lam_init)

    half = q_ref.shape[0] // 2
    halves = (pl.ds(0, half), pl.ds(half, half))
    scores = []
    for rows in halves:
        q = q_ref[rows, :]
        zero = jnp.zeros_like(q)
        scores.append((_dot_nt(jnp.where(lane < B_QK_DIM, q, zero), k),
                       _dot_nt(jnp.where(lane >= B_QK_DIM, q, zero), k)))

    def softmax(s):
        p = jnp.exp2(s - jnp.max(s, axis=-1, keepdims=True))
        return p, 1.0 / jnp.sum(p, axis=-1, keepdims=True)

    for rows, (s1, s2) in zip(halves, scores):
        p1, r1 = softmax(s1)
        p2, r2 = softmax(s2)
        o = _dot((p1 * r1 - p2 * (lam * r2)).astype(BF16), v)
        o = o * lax.rsqrt(jnp.mean(o * o, axis=-1, keepdims=True) + LN_EPS) * g_ref[...] * (1.0 - lam_init)
        o_ref[rows, :] = o.astype(BF16)


def _attn_b(proj, lam_vecs, subln_g, consts, n_seq, S, row0, tq):
    blk0 = row0 // S
    qb0 = row0 // tq
    n_qt = S // tq
    col = lambda part: (3 * A_QKV + part * B_COLS) // LANES
    return pl.pallas_call(
        _attn_b_kernel,
        grid=(n_seq, B_HEADS, n_qt),
        in_specs=[
            pl.BlockSpec((tq, LANES), lambda b, h, i: (qb0 + b * n_qt + i, col(0) + h)),
            pl.BlockSpec((S, LANES), lambda b, h, i: (blk0 + b, col(1) + h)),
            pl.BlockSpec((S, LANES), lambda b, h, i: (blk0 + b, col(2) + h)),
            pl.BlockSpec((4, B_QK_DIM), lambda b, h, i: (0, 0)),
            pl.BlockSpec((1, B_V_DIM), lambda b, h, i: (0, 0)),
            pl.BlockSpec((1, LANES), lambda b, h, i: (0, 0)),
        ],
        out_specs=pl.BlockSpec((tq, LANES), lambda b, h, i: (b * n_qt + i, h)),
        out_shape=jax.ShapeDtypeStruct((n_seq * S, B_COLS), BF16),
        compiler_params=_cparams(("parallel", "parallel", "arbitrary")),
    )(proj, proj, proj, lam_vecs, subln_g.reshape(1, B_V_DIM), consts)


N_ATT = 7


def _post_kernel(*refs, alpha, with_router, tile_starts):
    n_trunks = len(tile_starts)
    att, rest = refs[:N_ATT * n_trunks], refs[N_ATT * n_trunks:]
    ga_ref, gb_ref, x_ref, wa_ref, wb_ref, wo_ref, g_ref, b_ref = rest[:8]
    if with_router:
        wrh_ref, wrl_ref, xf_ref, xb_ref, lg_ref = rest[8:]
    else:
        xf_ref, xb_ref = rest[8:]

    def body(o1_ref, o2_ref, o3_ref, l1_ref, l2_ref, l3_ref, ob_ref):
        half = x_ref.shape[0] // 2
        halves = (pl.ds(0, half), pl.ds(half, half))
        branches = []
        for rows in halves:
            l1, l2, l3 = l1_ref[rows, :], l2_ref[rows, :], l3_ref[rows, :]
            m = jnp.maximum(jnp.maximum(l1, l2), l3)
            e1, e2, e3 = jnp.exp(l1 - m), jnp.exp(l2 - m), jnp.exp(l3 - m)
            oa = (e1 * o1_ref[rows, :].astype(F32) + e2 * o2_ref[rows, :].astype(F32)
                  + e3 * o3_ref[rows, :].astype(F32)) / (e1 + e2 + e3)
            branches.append((_dot(oa.astype(BF16), wa_ref[...]), _dot(ob_ref[rows, :], wb_ref[...])))
        zs = []
        for rows, (ya, yb) in zip(halves, branches):
            mix = (jax.nn.sigmoid(ga_ref[rows, :].astype(F32)) * ya
                   + jax.nn.sigmoid(gb_ref[rows, :].astype(F32)) * yb)
            zs.append(_dot(mix.astype(BF16), wo_ref[...]))
        for rows, z in zip(halves, zs):
            y = _layer_norm(alpha * x_ref[rows, :] + z, g_ref[...], b_ref[...])
            yh = y.astype(BF16)
            xf_ref[rows, :] = y
            xb_ref[rows, :] = yh
            if with_router:
                yl = (y - yh.astype(F32)).astype(BF16)
                lg_ref[rows, :] = _dot(yh, wrh_ref[...]) + (_dot(yl, wrh_ref[...]) + _dot(yh, wrl_ref[...]))

    i = pl.program_id(0)
    ends = tile_starts[1:] + (pl.num_programs(0),)
    for t in range(n_trunks):
        @pl.when((i >= tile_starts[t]) & (i < ends[t]))
        def _():
            body(*att[N_ATT * t:N_ATT * (t + 1)])


def _post(att, proj, x, wa, wb, wo, l, g, b, wr, alpha, tm):
    T, D = x.shape
    assert GATE_COL0 % D == 0
    gcol = GATE_COL0 // D
    row = lambda w: pl.BlockSpec((tm, w), lambda i: (i, 0))
    full = lambda a: pl.BlockSpec(a.shape, lambda i: (0,) * a.ndim)
    layer = lambda a: pl.BlockSpec((None,) + a.shape[1:], lambda i: (l, 0, 0))
    g2, b2 = g.reshape(1, D), b.reshape(1, D)
    in_specs, args, tile_starts, t0 = [], [], [], 0
    for arrays in att:
        n_t = arrays[0].shape[0] // tm
        for a in arrays:
            in_specs.append(pl.BlockSpec((tm, a.shape[1]), lambda i, t0=t0, n_t=n_t: (jnp.clip(i - t0, 0, n_t - 1), 0)))
            args.append(a)
        tile_starts.append(t0)
        t0 += n_t
    assert t0 == T // tm
    in_specs += [pl.BlockSpec((tm, D), lambda i: (i, gcol)),
                 pl.BlockSpec((tm, D), lambda i: (i, gcol + 1)),
                 row(D), layer(wa), layer(wb), layer(wo), full(g2), full(b2)]
    args += [proj, proj, x, wa, wb, wo, g2, b2]
    out_specs = [row(D), row(D)]
    out_shape = [jax.ShapeDtypeStruct((T, D), F32), jax.ShapeDtypeStruct((T, D), BF16)]
    if wr is not None:
        wr_hi = wr.astype(BF16)
        wr_lo = (wr - wr_hi.astype(F32)).astype(BF16)
        in_specs += [full(wr_hi), full(wr_lo)]
        args += [wr_hi, wr_lo]
        out_specs.append(row(LANES))
        out_shape.append(jax.ShapeDtypeStruct((T, LANES), F32))
    return pl.pallas_call(
        functools.partial(_post_kernel, alpha=alpha, with_router=wr is not None, tile_starts=tuple(tile_starts)),
        grid=(T // tm,),
        in_specs=in_specs,
        out_specs=out_specs,
        out_shape=out_shape,
        compiler_params=_cparams(("parallel",)),
    )(*args)


def _ffn_kernel(xb_ref, x_ref, wg_ref, wu_ref, wd_ref, g_ref, b_ref, xf_ref, xo_ref, acc_ref, *, alpha):
    f = pl.program_id(1)

    @pl.when(f == 0)
    def _():
        acc_ref[...] = jnp.zeros_like(acc_ref)

    xb = xb_ref[...]
    h = jax.nn.silu(_dot(xb, wg_ref[...])) * _dot(xb, wu_ref[...])
    acc_ref[...] += _dot(h.astype(BF16), wd_ref[...])

    @pl.when(f == pl.num_programs(1) - 1)
    def _():
        y = _layer_norm(alpha * x_ref[...] + acc_ref[...], g_ref[...], b_ref[...])
        xf_ref[...] = y
        xo_ref[...] = y.astype(BF16)


def _ffn(xb, x, wg, wu, wd, li, g, b, alpha, tm, tf):
    T, D = x.shape
    F = wg.shape[2]
    row = pl.BlockSpec((tm, D), lambda i, f: (i, 0))
    vec = pl.BlockSpec((1, D), lambda i, f: (0, 0))
    return pl.pallas_call(
        functools.partial(_ffn_kernel, alpha=alpha),
        grid=(T // tm, F // tf),
        in_specs=[row, row,
                  pl.BlockSpec((None, D, tf), lambda i, f: (li, 0, f)),
                  pl.BlockSpec((None, D, tf), lambda i, f: (li, 0, f)),
                  pl.BlockSpec((None, tf, D), lambda i, f: (li, f, 0)),
                  vec, vec],
        out_specs=[row, row],
        out_shape=[jax.ShapeDtypeStruct((T, D), F32), jax.ShapeDtypeStruct((T, D), BF16)],
        scratch_shapes=[pltpu.VMEM((tm, D), F32)],
        compiler_params=_cparams(("parallel", "arbitrary")),
    )(xb, x, wg, wu, wd, g.reshape(1, D), b.reshape(1, D))


def _route(logits, tm):
    T = logits.shape[0]
    top_logits, top_idx = lax.top_k(logits[:, :N_EXPERTS], TOP_K)
    gates = jax.nn.softmax(top_logits, axis=-1)
    n_assign = T * TOP_K
    flat_e = top_idx.reshape(n_assign).astype(jnp.int32)
    onehot = (flat_e[:, None] == jnp.arange(N_EXPERTS, dtype=jnp.int32)[None, :]).astype(jnp.int32)
    csum = jnp.cumsum(onehot, axis=0)
    counts = csum[-1]
    rank = jnp.sum((csum - onehot) * onehot, axis=1)
    padded = (counts + tm - 1) // tm * tm
    pad_end = jnp.cumsum(padded)
    pad_start = pad_end - padded
    dest = pad_start[flat_e] + rank
    n_blocks = -(-(n_assign + N_EXPERTS * (tm - 1)) // tm)
    n_rows = n_blocks * tm
    row_assign = jnp.full((n_rows,), -1, jnp.int32).at[dest].set(jnp.arange(n_assign, dtype=jnp.int32))
    used = row_assign >= 0
    row_src = jnp.where(used, row_assign // TOP_K, 0)
    spare = TOP_K * T + jnp.arange(n_rows, dtype=jnp.int32) % tm
    row_dst = jnp.where(used, (row_assign % TOP_K) * T + row_assign // TOP_K, spare)
    row_gate = jnp.where(used, gates.reshape(n_assign)[jnp.maximum(row_assign, 0)], 0.0)
    block_start = jnp.arange(n_blocks, dtype=jnp.int32) * tm
    block_expert = jnp.minimum(jnp.searchsorted(pad_end, block_start, side='right'), N_EXPERTS - 1).astype(jnp.int32)
    n_valid = (pad_end[-1] // tm).astype(jnp.int32).reshape(1)
    return block_expert, n_valid, row_src, row_dst, jnp.broadcast_to(row_gate[:, None], (n_rows, LANES))


def _moe_kernel(be_ref, nv_ref, src_ref, dst_ref, x_hbm, gate_ref, wg_ref, wu_ref, wd_ref, y_hbm,
                xg_ref, xb_ref, acc_ref, yo_ref, sem_in, sem_out, *, tm, chunk):
    i = pl.program_id(0)
    f = pl.program_id(1)
    n_f = pl.num_programs(1)
    n_blk = pl.num_programs(0)
    n_live = nv_ref[0]
    live = i < n_live
    slot = i % 2
    n_chunks = tm // chunk

    def gather_chunk(blk, buf, c):
        for r in range(c * chunk, (c + 1) * chunk):
            tok = src_ref[blk * tm + r]
            pltpu.make_async_copy(x_hbm.at[pl.ds(tok, 1)], xg_ref.at[buf, pl.ds(r, 1)], sem_in.at[buf]).start()

    def scatter_chunk(blk, c):
        for r in range(c * chunk, (c + 1) * chunk):
            dst = dst_ref[blk * tm + r]
            pltpu.make_async_copy(yo_ref.at[pl.ds(r, 1)], y_hbm.at[pl.ds(dst, 1)], sem_out).start()

    def wait_scatter():
        pltpu.make_async_copy(yo_ref, yo_ref, sem_out).wait()

    @pl.when((i == 0) & (f == 0))
    def _():
        for c in range(n_chunks):
            gather_chunk(0, 0, c)
        yo_ref[...] = jnp.zeros_like(yo_ref)
        spare = pltpu.make_async_copy(yo_ref, y_hbm.at[pl.ds(y_hbm.shape[0] - tm, tm)], sem_out)
        spare.start()
        spare.wait()

    @pl.when(live & (f == 0))
    def _():
        pltpu.make_async_copy(xg_ref.at[slot], xg_ref.at[slot], sem_in.at[slot]).wait()
        xb_ref[...] = xg_ref[slot].astype(BF16)
        acc_ref[...] = jnp.zeros_like(acc_ref)

    for c in range(n_chunks):
        @pl.when((f == c) & (i + 1 < n_live))
        def _():
            gather_chunk(i + 1, 1 - slot, c)

        @pl.when((f == c) & (i >= 1) & (i <= n_live))
        def _():
            scatter_chunk(i - 1, c)

    @pl.when(live)
    def _():
        xb = xb_ref[...]
        h = jax.nn.silu(_dot(xb, wg_ref[0])) * _dot(xb, wu_ref[0])
        acc_ref[...] += _dot(h.astype(BF16), wd_ref[0])

    @pl.when((f == n_f - 1) & (i >= 1) & (i <= n_live))
    def _():
        wait_scatter()

    @pl.when(live & (f == n_f - 1))
    def _():
        gate = gate_ref[...]
        for c in range(acc_ref.shape[1] // LANES):
            sl = slice(c * LANES, (c + 1) * LANES)
            yo_ref[:, sl] = acc_ref[:, sl] * gate

        @pl.when(i == n_blk - 1)
        def _():
            for c in range(n_chunks):
                scatter_chunk(i, c)
            wait_scatter()


def _moe(x, route, wg, wu, wd, li, tm, tf):
    T, D = x.shape
    F = wg.shape[2]
    nf = F // tf
    block_expert, n_valid, row_src, row_dst, row_gate = route
    n_blocks = block_expert.shape[0]
    chunk = next(c for c in range(8, tm + 1, 8) if tm % c == 0 and tm // c <= nf)

    def fidx(i, f, nv):
        return jnp.where(i < nv[0], f, nf - 1)

    grid_spec = pltpu.PrefetchScalarGridSpec(
        num_scalar_prefetch=4,
        grid=(n_blocks, nf),
        in_specs=[
            pl.BlockSpec(memory_space=pl.ANY),
            pl.BlockSpec((tm, LANES), lambda i, f, be, nv, rs, rd: (i, 0)),
            pl.BlockSpec((1, D, tf), lambda i, f, be, nv, rs, rd: (li * N_EXPERTS + be[i], 0, fidx(i, f, nv))),
            pl.BlockSpec((1, D, tf), lambda i, f, be, nv, rs, rd: (li * N_EXPERTS + be[i], 0, fidx(i, f, nv))),
            pl.BlockSpec((1, tf, D), lambda i, f, be, nv, rs, rd: (li * N_EXPERTS + be[i], fidx(i, f, nv), 0)),
        ],
        out_specs=pl.BlockSpec(memory_space=pl.ANY),
        scratch_shapes=[pltpu.VMEM((2, tm, D), F32), pltpu.VMEM((tm, D), BF16), pltpu.VMEM((tm, D), F32),
                        pltpu.VMEM((tm, D), F32), pltpu.SemaphoreType.DMA((2,)), pltpu.SemaphoreType.DMA(())],
    )
    y = pl.pallas_call(
        functools.partial(_moe_kernel, tm=tm, chunk=chunk),
        grid_spec=grid_spec,
        out_shape=jax.ShapeDtypeStruct((TOP_K * T + tm, D), F32),
        compiler_params=_cparams(("arbitrary", "arbitrary")),
    )(block_expert, n_valid, row_src, row_dst, x, row_gate, wg, wu, wd)
    return y


def _combine_kernel(x_ref, y0_ref, y1_ref, g_ref, b_ref, xf_ref, *rest, alpha):
    y = _layer_norm(alpha * x_ref[...] + (y0_ref[...] + y1_ref[...]), g_ref[...], b_ref[...])
    xf_ref[...] = y
    for xb_ref in rest:
        xb_ref[...] = y.astype(BF16)


def _combine(x, y, g, b, alpha, tm, row0, n_rows, with_bf16):
    T, D = x.shape
    n = T // tm
    i0 = row0 // tm
    row = pl.BlockSpec((tm, D), lambda i: (i, 0))
    vec = pl.BlockSpec((1, D), lambda i: (0, 0))
    out_specs = [row, row] if with_bf16 else [row]
    out_shape = [jax.ShapeDtypeStruct((n_rows, D), F32)]
    if with_bf16:
        out_shape.append(jax.ShapeDtypeStruct((n_rows, D), BF16))
    return pl.pallas_call(
        functools.partial(_combine_kernel, alpha=alpha),
        grid=(n_rows // tm,),
        in_specs=[pl.BlockSpec((tm, D), lambda i: (i0 + i, 0)),
                  pl.BlockSpec((tm, D), lambda i: (i0 + i, 0)),
                  pl.BlockSpec((tm, D), lambda i: (n + i0 + i, 0)), vec, vec],
        out_specs=out_specs,
        out_shape=out_shape,
        compiler_params=_cparams(("parallel",)),
    )(x, y, y, g.reshape(1, D), b.reshape(1, D))


def _tiles(D):
    big = D >= 2048
    return dict(ln=512, proj=2048, attn_b=512, post=256 if big else 512, ffn_m=512, ffn_f=512,
                moe_m=512, moe_f=1024, comb=512)


def kernel(x_prompt, x_sample, ln_in_g, ln_in_b, w_in, w_branch_a, w_branch_b, w_out, diff_lambda, diff_subln_g, ln_mix_g, ln_mix_b, ln_ffn_g, ln_ffn_b, ffn_w_gate, ffn_w_up, ffn_w_down, moe_router, moe_w_gate, moe_w_up, moe_w_down):
    depth = w_in.shape[0]
    D = x_prompt.shape[-1]
    alpha = (2 * depth) ** 0.25
    tl = _tiles(D)
    seqs = [(x_prompt.shape[0], x_prompt.shape[1]), (x_sample.shape[0], x_sample.shape[1])]
    x = jnp.concatenate([x_prompt.reshape(-1, D), x_sample.reshape(-1, D)], axis=0)
    T = x.shape[0]
    tab = _rope_tables(max(S for _, S in seqs))
    bf = lambda w: w.astype(BF16)
    w_in, w_branch_a, w_branch_b, w_out = bf(w_in), bf(w_branch_a), bf(w_branch_b), bf(w_out)
    ffn_w_gate, ffn_w_up, ffn_w_down = bf(ffn_w_gate), bf(ffn_w_up), bf(ffn_w_down)
    merge = lambda w: bf(w).reshape((-1,) + w.shape[2:])
    moe_w_gate, moe_w_up, moe_w_down = merge(moe_w_gate), merge(moe_w_up), merge(moe_w_down)

    xf, xb = _ln_in(x, ln_in_g, ln_in_b, tl['ln'])
    for l in range(depth):
        lam_init = 0.8 - 0.6 * math.exp(-0.3 * l)
        consts = jnp.full((1, LANES), lam_init, F32)
        proj = _proj(xb, w_in, l, tab, seqs, tl['proj'])

        att, row0 = [], 0
        for n_seq, S in seqs:
            groups = [_attn_a_group(proj, g, n_seq, S, row0) for g in range(len(A_GROUPS))]
            ob = _attn_b(proj, diff_lambda[l], diff_subln_g[l], consts, n_seq, S, row0, tl['attn_b'])
            att.append([o for o, _ in groups] + [lse for _, lse in groups] + [ob])
            row0 += n_seq * S

        moe_layer = l % 2 == 1
        i = l // 2
        wr = None
        if moe_layer:
            wr = jnp.zeros((D, LANES), F32).at[:, :N_EXPERTS].set(moe_router[i])
        res = _post(att, proj, xf, w_branch_a, w_branch_b, w_out, l,
                    ln_mix_g[l], ln_mix_b[l], wr, alpha, tl['post'])
        if moe_layer:
            xf, xb, logits = res
            route = _route(logits, tl['moe_m'])
            y = _moe(xf, route, moe_w_gate, moe_w_up, moe_w_down, i, tl['moe_m'], tl['moe_f'])
            if l == depth - 1:
                n_p = seqs[0][0] * seqs[0][1]
                outs = [_combine(xf, y, ln_ffn_g[l], ln_ffn_b[l], alpha, tl['comb'], r0, n, False)[0]
                        for r0, n in ((0, n_p), (n_p, T - n_p))]
                return (outs[0].reshape(x_prompt.shape), outs[1].reshape(x_sample.shape))
            xf, xb = _combine(xf, y, ln_ffn_g[l], ln_ffn_b[l], alpha, tl['comb'], 0, T, True)
        else:
            xf, xb = res
            xf, xb = _ffn(xb, xf, ffn_w_gate, ffn_w_up, ffn_w_down, i,
                          ln_ffn_g[l], ln_ffn_b[l], alpha, tl['ffn_m'], tl['ffn_f'])

    n_p = seqs[0][0] * seqs[0][1]
    return (xf[:n_p].reshape(x_prompt.shape), xf[n_p:].reshape(x_sample.shape))
```

```python
import functools
import math

import jax
import jax.numpy as jnp
from jax import lax
from jax.experimental import pallas as pl
from jax.experimental.pallas import tpu as pltpu

F32 = jnp.float32
BF16 = jnp.bfloat16

HEAD_DIM = 128
A_GROUPS = ((128, 1), (512, 4), (2048, 16))
A_HEADS_PER_GROUP = 4
A_OUT = A_HEADS_PER_GROUP * HEAD_DIM
A_QKV = len(A_GROUPS) * A_OUT
B_HEADS = 4
B_QK_DIM = 64
B_V_DIM = 2 * B_QK_DIM
B_COLS = B_HEADS * B_V_DIM
ROPE_THETA = 500000.0
ROPE_FRACTION = 4
N_EXPERTS = 8
TOP_K = 2
LN_EPS = 1e-5
MASK_VALUE = -1e30
LOG2_E = math.log2(math.e)
GATE_COL0 = 3 * A_QKV + 3 * B_COLS

LANES = 128
COL_BLOCK = 512
VMEM_LIMIT = 56 * 1024 * 1024


def _cparams(sem):
    return pltpu.CompilerParams(dimension_semantics=sem, vmem_limit_bytes=VMEM_LIMIT)


def _layer_norm(y, g, b):
    mu = jnp.mean(y, axis=-1, keepdims=True)
    yc = y - mu
    var = jnp.mean(yc * yc, axis=-1, keepdims=True)
    return yc * lax.rsqrt(var + LN_EPS) * g + b


def _dot(a, b):
    return jnp.dot(a, b, preferred_element_type=F32)


def _dot_nt(a, b):
    return lax.dot_general(a, b, (((1,), (1,)), ((), ())), preferred_element_type=F32)


def _ln_in_kernel(*refs, tile_starts):
    n_trunks = len(tile_starts)
    g_ref, b_ref, xf_ref, xb_ref = refs[n_trunks:]
    i = pl.program_id(0)
    ends = tile_starts[1:] + (pl.num_programs(0),)
    for t in range(n_trunks):
        @pl.when((i >= tile_starts[t]) & (i < ends[t]))
        def _():
            y = _layer_norm(refs[t][...], g_ref[...], b_ref[...])
            xf_ref[...] = y
            xb_ref[...] = y.astype(BF16)


def _ln_in(xs, g, b, tm):
    D = xs[0].shape[1]
    in_specs, tile_starts, t0 = [], [], 0
    for x in xs:
        n_t = x.shape[0] // tm
        in_specs.append(pl.BlockSpec((tm, D), lambda i, t0=t0, n_t=n_t: (jnp.clip(i - t0, 0, n_t - 1), 0)))
        tile_starts.append(t0)
        t0 += n_t
    row = pl.BlockSpec((tm, D), lambda i: (i, 0))
    vec = pl.BlockSpec((1, D), lambda i: (0, 0))
    return pl.pallas_call(
        functools.partial(_ln_in_kernel, tile_starts=tuple(tile_starts)),
        grid=(t0,),
        in_specs=in_specs + [vec, vec],
        out_specs=[row, row],
        out_shape=[jax.ShapeDtypeStruct((t0 * tm, D), F32), jax.ShapeDtypeStruct((t0 * tm, D), BF16)],
        compiler_params=_cparams(("parallel",)),
    )(*xs, g.reshape(1, D), b.reshape(1, D))


def _rope_tables(s_max):
    pos = jnp.arange(s_max, dtype=F32)[:, None]

    def tables(head, n_rep):
        rot = head // ROPE_FRACTION
        half = rot // 2
        inv_freq = ROPE_THETA ** (-jnp.arange(half, dtype=F32) / half)
        ang = pos * inv_freq[None, :]
        cos, sin = jnp.cos(ang), jnp.sin(ang)
        zero = jnp.zeros((s_max, head - rot), F32)
        zh = jnp.zeros((s_max, half), F32)
        c = jnp.concatenate([cos, cos, jnp.ones((s_max, head - rot), F32)], axis=1)
        s_up = jnp.concatenate([-sin, zh, zero], axis=1)
        s_dn = jnp.concatenate([zh, sin, zero], axis=1)
        return [jnp.tile(t, (1, n_rep)) for t in (c, s_up, s_dn)]

    return jnp.stack(tables(HEAD_DIM, 1) + tables(B_QK_DIM, 2), axis=0)


def _proj_kernel(x_ref, w_ref, tab_ref, o_ref):
    j = pl.program_id(1)
    n_a = A_QKV // COL_BLOCK
    is_a = j < 2 * n_a
    jb = j - 3 * n_a
    is_b = (jb == 0) | (jb == 1)

    def rope(t0, half, scale):
        n_parts = 4
        part = x_ref.shape[0] // n_parts
        parts = [pl.ds(a * part, part) for a in range(n_parts)]
        accs = [_dot(x_ref[rows, :], w_ref[...]) for rows in parts]
        for rows, acc in zip(parts, accs):
            for c in range(COL_BLOCK // LANES):
                sl = slice(c * LANES, (c + 1) * LANES)
                xc = acc[:, sl]
                up = pltpu.roll(xc, LANES - half, 1)
                dn = pltpu.roll(xc, half, 1)
                y = xc * tab_ref[t0, rows, :] + up * tab_ref[t0 + 1, rows, :] + dn * tab_ref[t0 + 2, rows, :]
                o_ref[rows, sl] = (y * scale).astype(BF16)

    @pl.when(is_a)
    def _():
        rope(0, HEAD_DIM // ROPE_FRACTION // 2, jnp.where(j < n_a, HEAD_DIM ** -0.5, 1.0).astype(F32))

    @pl.when(is_b)
    def _():
        rope(3, B_QK_DIM // ROPE_FRACTION // 2, jnp.where(jb == 0, B_QK_DIM ** -0.5 * LOG2_E, 1.0).astype(F32))

    @pl.when(jnp.logical_not(is_a | is_b))
    def _():
        o_ref[...] = _dot(x_ref[...], w_ref[...]).astype(BF16)


def _proj(xb, w, l, tab, seqs, tm):
    T, D = xb.shape
    C = w.shape[2]
    bounds = []
    t0 = 0
    for n_seq, S in seqs:
        assert S % tm == 0
        bounds.append((t0 // tm, S // tm))
        t0 += n_seq * S
    assert t0 == T

    def pos_block(i):
        blk = (i - bounds[0][0]) % bounds[0][1]
        for first, per in bounds[1:]:
            blk = jnp.where(i >= first, (i - first) % per, blk)
        return blk

    return pl.pallas_call(
        _proj_kernel,
        grid=(T // tm, C // COL_BLOCK),
        in_specs=[
            pl.BlockSpec((tm, D), lambda i, j: (i, 0)),
            pl.BlockSpec((None, D, COL_BLOCK), lambda i, j: (l, 0, j)),
            pl.BlockSpec((6, tm, LANES), lambda i, j: (0, pos_block(i), 0)),
        ],
        out_specs=pl.BlockSpec((tm, COL_BLOCK), lambda i, j: (i, j)),
        out_shape=jax.ShapeDtypeStruct((T, C), BF16),
        compiler_params=_cparams(("parallel", "arbitrary")),
    )(xb, w, tab)


A_TILES_IN_FLIGHT = 8


def _attend_tiles(tiles, n_side):
    scores = [_dot_nt(q, k) for q, k, _, _, _ in tiles]
    probs, stats = [], []
    for s, (_, _, _, q0, start) in zip(scores, tiles):
        tq, kw = s.shape
        qpos = q0 + lax.broadcasted_iota(jnp.int32, (tq, 1), 0)
        kpos = start + lax.broadcasted_iota(jnp.int32, (1, kw), 1)
        s = jnp.where(jnp.abs(kpos - qpos) <= n_side, s, MASK_VALUE)
        m = jnp.max(s, axis=-1, keepdims=True)
        p = jnp.exp(s - m)
        probs.append(p.astype(BF16))
        stats.append((m, jnp.sum(p, axis=-1, keepdims=True)))
    outs = [_dot(p, v) for p, (_, _, v, _, _) in zip(probs, tiles)]
    return [(o / den, m + jnp.log(den)) for o, (m, den) in zip(outs, stats)]


def _attn_a_kernel(q_ref, k_ref, v_ref, o_ref, lse_ref, *, L, tq, kw, n_side):
    n_q = L // tq
    group = max(1, A_TILES_IN_FLIGHT // A_HEADS_PER_GROUP)
    assert n_q % group == 0

    def body(it, carry):
        where, tiles = [], []
        for t in range(group):
            q0 = pl.multiple_of((it * group + t) * tq, tq)
            start = pl.multiple_of(jnp.clip(q0 - n_side, 0, L - kw), n_side)
            for h in range(A_HEADS_PER_GROUP):
                sl = slice(h * HEAD_DIM, (h + 1) * HEAD_DIM)
                where.append((q0, sl))
                tiles.append((q_ref[pl.ds(q0, tq), sl], k_ref[pl.ds(start, kw), sl],
                              v_ref[pl.ds(start, kw), sl], q0, start))
        for (q0, sl), (o, lse) in zip(where, _attend_tiles(tiles, n_side)):
            o_ref[pl.ds(q0, tq), sl] = o.astype(BF16)
            lse_ref[pl.ds(q0, tq), sl] = jnp.broadcast_to(lse, (tq, HEAD_DIM))
        return carry

    lax.fori_loop(0, n_q // group, body, 0)


def _attn_a_dil_kernel(q_ref, k_ref, v_ref, o_ref, lse_ref, qf_ref, kf_ref, vf_ref, *, S, d, tq, kw, n_side, classes):
    L = S // d
    qf_ref[...] = q_ref[...].astype(F32)
    kf_ref[...] = k_ref[...].astype(F32)
    vf_ref[...] = v_ref[...].astype(F32)

    def rows(first, n, r):
        return pl.ds(r + first * d, n, stride=d)

    def body(it, carry):
        where, tiles = [], []
        for u in range(classes):
            r = it * classes + u
            for qi in range(L // tq):
                q0 = qi * tq
                start = min(max(q0 - n_side, 0), L - kw)
                where.append(rows(q0, tq, r))
                tiles.append((qf_ref[rows(q0, tq, r), :].astype(BF16), kf_ref[rows(start, kw, r), :].astype(BF16),
                              vf_ref[rows(start, kw, r), :].astype(BF16), q0, start))
        for idx, (o, lse) in zip(where, _attend_tiles(tiles, n_side)):
            o_ref[idx, :] = o
            lse_ref[idx, :] = jnp.broadcast_to(lse, (tq, HEAD_DIM))
        return carry

    lax.fori_loop(0, d // classes, body, 0)


def _attn_a_group(proj, g, n_seq, S, row0):
    window, d = A_GROUPS[g]
    n_side = window // (2 * d)
    L = S // d
    tq = min(128, L)
    kw = min(L, tq + 2 * n_side)
    assert L % tq == 0 and tq % n_side == 0 and (L - kw) % n_side == 0
    n_g = len(A_GROUPS)
    blk0 = row0 // S
    o_dtype = BF16 if d == 1 else F32
    out_shape = [jax.ShapeDtypeStruct((n_seq * S, A_OUT), o_dtype), jax.ShapeDtypeStruct((n_seq * S, A_OUT), F32)]
    if d == 1:
        spec = lambda part: pl.BlockSpec((S, COL_BLOCK), lambda b: (blk0 + b, part * n_g + g))
        out_spec = pl.BlockSpec((S, A_OUT), lambda b: (b, 0))
        return pl.pallas_call(
            functools.partial(_attn_a_kernel, L=L, tq=tq, kw=kw, n_side=n_side),
            grid=(n_seq,),
            in_specs=[spec(0), spec(1), spec(2)],
            out_specs=[out_spec, out_spec],
            out_shape=out_shape,
            compiler_params=_cparams(("parallel",)),
        )(proj, proj, proj)
    heads = n_g * A_HEADS_PER_GROUP
    spec = lambda part: pl.BlockSpec((S, HEAD_DIM), lambda b, h: (blk0 + b, part * heads + g * A_HEADS_PER_GROUP + h))
    out_spec = pl.BlockSpec((S, HEAD_DIM), lambda b, h: (b, h))
    classes = max(1, min(d, A_TILES_IN_FLIGHT // (L // tq)))
    assert d % classes == 0
    return pl.pallas_call(
        functools.partial(_attn_a_dil_kernel, S=S, d=d, tq=tq, kw=kw, n_side=n_side, classes=classes),
        grid=(n_seq, A_HEADS_PER_GROUP),
        in_specs=[spec(0), spec(1), spec(2)],
        out_specs=[out_spec, out_spec],
        out_shape=out_shape,
        scratch_shapes=[pltpu.VMEM((S, HEAD_DIM), F32)] * 3,
        compiler_params=_cparams(("parallel", "parallel")),
    )(proj, proj, proj)


B_ROW_CHUNK = 256
B_SCORE_BYTES = 16 << 20


def _attn_b_kernel(q_ref, k_ref, v_ref, lam_ref, g_ref, c_ref, o_ref):
    k = k_ref[...]
    v = v_ref[...]
    lane = lax.broadcasted_iota(jnp.int32, (1, 2 * B_QK_DIM), 1)
    lam_init = c_ref[0:1, 0:1]
    lv = lam_ref[...]
    lam = (jnp.exp(jnp.sum(lv[0:1] * lv[1:2], axis=-1, keepdims=True))
           - jnp.exp(jnp.sum(lv[2:3] * lv[3:4], axis=-1, keepdims=True)) + lam_init)

    chunks = [pl.ds(a, B_ROW_CHUNK) for a in range(0, q_ref.shape[0], B_ROW_CHUNK)]
    scores = []
    for rows in chunks:
        q = q_ref[rows, :]
        zero = jnp.zeros_like(q)
        scores.append((_dot_nt(jnp.where(lane < B_QK_DIM, q, zero), k),
                       _dot_nt(jnp.where(lane >= B_QK_DIM, q, zero), k)))

    def softmax(s):
        p = jnp.exp2(s - jnp.max(s, axis=-1, keepdims=True))
        return p, jnp.sum(p, axis=-1, keepdims=True)

    for rows, (s1, s2) in zip(chunks, scores):
        p1, den1 = softmax(s1)
        p2, den2 = softmax(s2)
        o = _dot((p1 - p2 * (lam * den1 / den2)).astype(BF16), v) / den1
        o = o * lax.rsqrt(jnp.mean(o * o, axis=-1, keepdims=True) + LN_EPS) * g_ref[...] * (1.0 - lam_init)
        o_ref[rows, :] = o.astype(BF16)


def _attn_b(proj, lam_vecs, subln_g, consts, n_seq, S, row0):
    tq = min(S, B_ROW_CHUNK * max(1, B_SCORE_BYTES // (2 * B_ROW_CHUNK * S * 4)))
    assert S % tq == 0 and tq % B_ROW_CHUNK == 0
    blk0 = row0 // S
    qb0 = row0 // tq
    n_qt = S // tq
    col = lambda part: (3 * A_QKV + part * B_COLS) // LANES
    return pl.pallas_call(
        _attn_b_kernel,
        grid=(n_seq, B_HEADS, n_qt),
        in_specs=[
            pl.BlockSpec((tq, LANES), lambda b, h, i: (qb0 + b * n_qt + i, col(0) + h)),
            pl.BlockSpec((S, LANES), lambda b, h, i: (blk0 + b, col(1) + h)),
            pl.BlockSpec((S, LANES), lambda b, h, i: (blk0 + b, col(2) + h)),
            pl.BlockSpec((4, B_QK_DIM), lambda b, h, i: (0, 0)),
            pl.BlockSpec((1, B_V_DIM), lambda b, h, i: (0, 0)),
            pl.BlockSpec((1, LANES), lambda b, h, i: (0, 0)),
        ],
        out_specs=pl.BlockSpec((tq, LANES), lambda b, h, i: (b * n_qt + i, h)),
        out_shape=jax.ShapeDtypeStruct((n_seq * S, B_COLS), BF16),
        compiler_params=_cparams(("parallel", "parallel", "arbitrary")),
    )(proj, proj, proj, lam_vecs, subln_g.reshape(1, B_V_DIM), consts)


N_ATT = 7


def _post_kernel(*refs, alpha, with_router, tile_starts):
    n_trunks = len(tile_starts)
    att, rest = refs[:N_ATT * n_trunks], refs[N_ATT * n_trunks:]
    ga_ref, gb_ref, x_ref, wa_ref, wb_ref, wo_ref, g_ref, b_ref = rest[:8]
    if with_router:
        wrh_ref, wrl_ref, xf_ref, xb_ref, lg_ref = rest[8:]
    else:
        xf_ref, xb_ref = rest[8:]

    def body(o1_ref, o2_ref, o3_ref, l1_ref, l2_ref, l3_ref, ob_ref):
        half = x_ref.shape[0] // 2
        halves = (pl.ds(0, half), pl.ds(half, half))
        branches = []
        for rows in halves:
            l1, l2, l3 = l1_ref[rows, :], l2_ref[rows, :], l3_ref[rows, :]
            m = jnp.maximum(jnp.maximum(l1, l2), l3)
            e1, e2, e3 = jnp.exp(l1 - m), jnp.exp(l2 - m), jnp.exp(l3 - m)
            oa = (e1 * o1_ref[rows, :].astype(F32) + e2 * o2_ref[rows, :].astype(F32)
                  + e3 * o3_ref[rows, :].astype(F32)) / (e1 + e2 + e3)
            branches.append((_dot(oa.astype(BF16), wa_ref[...]), _dot(ob_ref[rows, :], wb_ref[...])))
        zs = []
        for rows, (ya, yb) in zip(halves, branches):
            mix = (jax.nn.sigmoid(ga_ref[rows, :].astype(F32)) * ya
                   + jax.nn.sigmoid(gb_ref[rows, :].astype(F32)) * yb)
            zs.append(_dot(mix.astype(BF16), wo_ref[...]))
        for rows, z in zip(halves, zs):
            y = _layer_norm(alpha * x_ref[rows, :] + z, g_ref[...], b_ref[...])
            yh = y.astype(BF16)
            xf_ref[rows, :] = y
            xb_ref[rows, :] = yh
            if with_router:
                yl = (y - yh.astype(F32)).astype(BF16)
                lg_ref[rows, :] = _dot(yh, wrh_ref[...]) + (_dot(yl, wrh_ref[...]) + _dot(yh, wrl_ref[...]))

    i = pl.program_id(0)
    ends = tile_starts[1:] + (pl.num_programs(0),)
    for t in range(n_trunks):
        @pl.when((i >= tile_starts[t]) & (i < ends[t]))
        def _():
            body(*att[N_ATT * t:N_ATT * (t + 1)])


def _post(att, proj, x, wa, wb, wo, l, g, b, wr, alpha, tm):
    T, D = x.shape
    assert GATE_COL0 % D == 0
    gcol = GATE_COL0 // D
    row = lambda w: pl.BlockSpec((tm, w), lambda i: (i, 0))
    full = lambda a: pl.BlockSpec(a.shape, lambda i: (0,) * a.ndim)
    layer = lambda a: pl.BlockSpec((None,) + a.shape[1:], lambda i: (l, 0, 0))
    g2, b2 = g.reshape(1, D), b.reshape(1, D)
    in_specs, args, tile_starts, t0 = [], [], [], 0
    for arrays in att:
        n_t = arrays[0].shape[0] // tm
        for a in arrays:
            in_specs.append(pl.BlockSpec((tm, a.shape[1]), lambda i, t0=t0, n_t=n_t: (jnp.clip(i - t0, 0, n_t - 1), 0)))
            args.append(a)
        tile_starts.append(t0)
        t0 += n_t
    assert t0 == T // tm
    in_specs += [pl.BlockSpec((tm, D), lambda i: (i, gcol)),
                 pl.BlockSpec((tm, D), lambda i: (i, gcol + 1)),
                 row(D), layer(wa), layer(wb), layer(wo), full(g2), full(b2)]
    args += [proj, proj, x, wa, wb, wo, g2, b2]
    out_specs = [row(D), row(D)]
    out_shape = [jax.ShapeDtypeStruct((T, D), F32), jax.ShapeDtypeStruct((T, D), BF16)]
    if wr is not None:
        wr_hi = wr.astype(BF16)
        wr_lo = (wr - wr_hi.astype(F32)).astype(BF16)
        in_specs += [full(wr_hi), full(wr_lo)]
        args += [wr_hi, wr_lo]
        out_specs.append(row(LANES))
        out_shape.append(jax.ShapeDtypeStruct((T, LANES), F32))
    return pl.pallas_call(
        functools.partial(_post_kernel, alpha=alpha, with_router=wr is not None, tile_starts=tuple(tile_starts)),
        grid=(T // tm,),
        in_specs=in_specs,
        out_specs=out_specs,
        out_shape=out_shape,
        compiler_params=_cparams(("parallel",)),
    )(*args)


def _ffn_kernel(xb_ref, x_ref, wg_ref, wu_ref, wd_ref, g_ref, b_ref, xf_ref, xo_ref, acc_ref, *, alpha):
    f = pl.program_id(1)

    @pl.when(f == 0)
    def _():
        acc_ref[...] = jnp.zeros_like(acc_ref)

    xb = xb_ref[...]
    h = jax.nn.silu(_dot(xb, wg_ref[...])) * _dot(xb, wu_ref[...])
    acc_ref[...] += _dot(h.astype(BF16), wd_ref[...])

    @pl.when(f == pl.num_programs(1) - 1)
    def _():
        y = _layer_norm(alpha * x_ref[...] + acc_ref[...], g_ref[...], b_ref[...])
        xf_ref[...] = y
        xo_ref[...] = y.astype(BF16)


def _ffn(xb, x, wg, wu, wd, li, g, b, alpha, tm, tf):
    T, D = x.shape
    F = wg.shape[2]
    row = pl.BlockSpec((tm, D), lambda i, f: (i, 0))
    vec = pl.BlockSpec((1, D), lambda i, f: (0, 0))
    return pl.pallas_call(
        functools.partial(_ffn_kernel, alpha=alpha),
        grid=(T // tm, F // tf),
        in_specs=[row, row,
                  pl.BlockSpec((None, D, tf), lambda i, f: (li, 0, f)),
                  pl.BlockSpec((None, D, tf), lambda i, f: (li, 0, f)),
                  pl.BlockSpec((None, tf, D), lambda i, f: (li, f, 0)),
                  vec, vec],
        out_specs=[row, row],
        out_shape=[jax.ShapeDtypeStruct((T, D), F32), jax.ShapeDtypeStruct((T, D), BF16)],
        scratch_shapes=[pltpu.VMEM((tm, D), F32)],
        compiler_params=_cparams(("parallel", "arbitrary")),
    )(xb, x, wg, wu, wd, g.reshape(1, D), b.reshape(1, D))


def _route(logits, tm):
    T = logits.shape[0]
    top_logits, top_idx = lax.top_k(logits[:, :N_EXPERTS], TOP_K)
    gates = jax.nn.softmax(top_logits, axis=-1)
    n_assign = T * TOP_K
    flat_e = top_idx.reshape(n_assign).astype(jnp.int32)
    onehot = (flat_e[:, None] == jnp.arange(N_EXPERTS, dtype=jnp.int32)[None, :]).astype(jnp.int32)
    csum = jnp.cumsum(onehot, axis=0)
    counts = csum[-1]
    rank = jnp.sum((csum - onehot) * onehot, axis=1)
    padded = (counts + tm - 1) // tm * tm
    pad_end = jnp.cumsum(padded)
    pad_start = pad_end - padded
    dest = pad_start[flat_e] + rank
    n_blocks = -(-(n_assign + N_EXPERTS * (tm - 1)) // tm)
    n_rows = n_blocks * tm
    row_assign = jnp.full((n_rows,), -1, jnp.int32).at[dest].set(jnp.arange(n_assign, dtype=jnp.int32))
    used = row_assign >= 0
    row_src = jnp.where(used, row_assign // TOP_K, 0)
    spare = TOP_K * T + jnp.arange(n_rows, dtype=jnp.int32) % tm
    row_dst = jnp.where(used, (row_assign % TOP_K) * T + row_assign // TOP_K, spare)
    row_gate = jnp.where(used, gates.reshape(n_assign)[jnp.maximum(row_assign, 0)], 0.0)
    block_start = jnp.arange(n_blocks, dtype=jnp.int32) * tm
    block_expert = jnp.minimum(jnp.searchsorted(pad_end, block_start, side='right'), N_EXPERTS - 1).astype(jnp.int32)
    n_valid = (pad_end[-1] // tm).astype(jnp.int32).reshape(1)
    return block_expert, n_valid, row_src, row_dst, jnp.broadcast_to(row_gate[:, None], (n_rows, LANES))


def _moe_kernel(be_ref, nv_ref, src_ref, dst_ref, x_hbm, gate_ref, wg_ref, wu_ref, wd_ref, y_hbm,
                xg_ref, xb_ref, acc_ref, yo_ref, sem_in, sem_out, *, tm, chunk):
    i = pl.program_id(0)
    f = pl.program_id(1)
    n_f = pl.num_programs(1)
    n_blk = pl.num_programs(0)
    n_live = nv_ref[0]
    live = i < n_live
    slot = i % 2
    n_chunks = tm // chunk

    def gather_chunk(blk, buf, c):
        for r in range(c * chunk, (c + 1) * chunk):
            tok = src_ref[blk * tm + r]
            pltpu.make_async_copy(x_hbm.at[pl.ds(tok, 1)], xg_ref.at[buf, pl.ds(r, 1)], sem_in.at[buf]).start()

    def scatter_chunk(blk, c):
        for r in range(c * chunk, (c + 1) * chunk):
            dst = dst_ref[blk * tm + r]
            pltpu.make_async_copy(yo_ref.at[pl.ds(r, 1)], y_hbm.at[pl.ds(dst, 1)], sem_out).start()

    def wait_scatter():
        pltpu.make_async_copy(yo_ref, yo_ref, sem_out).wait()

    @pl.when((i == 0) & (f == 0))
    def _():
        for c in range(n_chunks):
            gather_chunk(0, 0, c)
        yo_ref[...] = jnp.zeros_like(yo_ref)
        spare = pltpu.make_async_copy(yo_ref, y_hbm.at[pl.ds(y_hbm.shape[0] - tm, tm)], sem_out)
        spare.start()
        spare.wait()

    @pl.when(live & (f == 0))
    def _():
        pltpu.make_async_copy(xg_ref.at[slot], xg_ref.at[slot], sem_in.at[slot]).wait()
        xb_ref[...] = xg_ref[slot].astype(BF16)
        acc_ref[...] = jnp.zeros_like(acc_ref)

    for c in range(n_chunks):
        @pl.when((f == c) & (i + 1 < n_live))
        def _():
            gather_chunk(i + 1, 1 - slot, c)

        @pl.when((f == c) & (i >= 1) & (i <= n_live))
        def _():
            scatter_chunk(i - 1, c)

    @pl.when(live)
    def _():
        xb = xb_ref[...]
        h = jax.nn.silu(_dot(xb, wg_ref[0])) * _dot(xb, wu_ref[0])
        acc_ref[...] += _dot(h.astype(BF16), wd_ref[0])

    @pl.when((f == n_f - 1) & (i >= 1) & (i <= n_live))
    def _():
        wait_scatter()

    @pl.when(live & (f == n_f - 1))
    def _():
        gate = gate_ref[...]
        for c in range(acc_ref.shape[1] // LANES):
            sl = slice(c * LANES, (c + 1) * LANES)
            yo_ref[:, sl] = acc_ref[:, sl] * gate

        @pl.when(i == n_blk - 1)
        def _():
            for c in range(n_chunks):
                scatter_chunk(i, c)
            wait_scatter()


def _moe(x, route, wg, wu, wd, li, tm, tf):
    T, D = x.shape
    F = wg.shape[2]
    nf = F // tf
    block_expert, n_valid, row_src, row_dst, row_gate = route
    n_blocks = block_expert.shape[0]
    chunk = next(c for c in range(8, tm + 1, 8) if tm % c == 0 and tm // c <= nf)

    def fidx(i, f, nv):
        return jnp.where(i < nv[0], f, nf - 1)

    grid_spec = pltpu.PrefetchScalarGridSpec(
        num_scalar_prefetch=4,
        grid=(n_blocks, nf),
        in_specs=[
            pl.BlockSpec(memory_space=pl.ANY),
            pl.BlockSpec((tm, LANES), lambda i, f, be, nv, rs, rd: (i, 0)),
            pl.BlockSpec((1, D, tf), lambda i, f, be, nv, rs, rd: (li * N_EXPERTS + be[i], 0, fidx(i, f, nv))),
            pl.BlockSpec((1, D, tf), lambda i, f, be, nv, rs, rd: (li * N_EXPERTS + be[i], 0, fidx(i, f, nv))),
            pl.BlockSpec((1, tf, D), lambda i, f, be, nv, rs, rd: (li * N_EXPERTS + be[i], fidx(i, f, nv), 0)),
        ],
        out_specs=pl.BlockSpec(memory_space=pl.ANY),
        scratch_shapes=[pltpu.VMEM((2, tm, D), F32), pltpu.VMEM((tm, D), BF16), pltpu.VMEM((tm, D), F32),
                        pltpu.VMEM((tm, D), F32), pltpu.SemaphoreType.DMA((2,)), pltpu.SemaphoreType.DMA(())],
    )
    y = pl.pallas_call(
        functools.partial(_moe_kernel, tm=tm, chunk=chunk),
        grid_spec=grid_spec,
        out_shape=jax.ShapeDtypeStruct((TOP_K * T + tm, D), F32),
        compiler_params=_cparams(("arbitrary", "arbitrary")),
    )(block_expert, n_valid, row_src, row_dst, x, row_gate, wg, wu, wd)
    return y


def _combine_kernel(x_ref, y0_ref, y1_ref, g_ref, b_ref, xf_ref, *rest, alpha):
    y = _layer_norm(alpha * x_ref[...] + (y0_ref[...] + y1_ref[...]), g_ref[...], b_ref[...])
    xf_ref[...] = y
    for xb_ref in rest:
        xb_ref[...] = y.astype(BF16)


def _combine(x, y, g, b, alpha, tm, row0, n_rows, with_bf16):
    T, D = x.shape
    n = T // tm
    i0 = row0 // tm
    row = pl.BlockSpec((tm, D), lambda i: (i, 0))
    vec = pl.BlockSpec((1, D), lambda i: (0, 0))
    out_specs = [row, row] if with_bf16 else [row]
    out_shape = [jax.ShapeDtypeStruct((n_rows, D), F32)]
    if with_bf16:
        out_shape.append(jax.ShapeDtypeStruct((n_rows, D), BF16))
    return pl.pallas_call(
        functools.partial(_combine_kernel, alpha=alpha),
        grid=(n_rows // tm,),
        in_specs=[pl.BlockSpec((tm, D), lambda i: (i0 + i, 0)),
                  pl.BlockSpec((tm, D), lambda i: (i0 + i, 0)),
                  pl.BlockSpec((tm, D), lambda i: (n + i0 + i, 0)), vec, vec],
        out_specs=out_specs,
        out_shape=out_shape,
        compiler_params=_cparams(("parallel",)),
    )(x, y, y, g.reshape(1, D), b.reshape(1, D))


def _tiles(D):
    big = D >= 2048
    return dict(ln=512, proj=2048, post=256 if big else 512, ffn_m=512, ffn_f=512,
                moe_m=512, moe_f=1024, comb=512)


def kernel(x_prompt, x_sample, ln_in_g, ln_in_b, w_in, w_branch_a, w_branch_b, w_out, diff_lambda, diff_subln_g, ln_mix_g, ln_mix_b, ln_ffn_g, ln_ffn_b, ffn_w_gate, ffn_w_up, ffn_w_down, moe_router, moe_w_gate, moe_w_up, moe_w_down):
    depth = w_in.shape[0]
    D = x_prompt.shape[-1]
    alpha = (2 * depth) ** 0.25
    tl = _tiles(D)
    seqs = [(x_prompt.shape[0], x_prompt.shape[1]), (x_sample.shape[0], x_sample.shape[1])]
    T = sum(n_seq * S for n_seq, S in seqs)
    tab = _rope_tables(max(S for _, S in seqs))
    bf = lambda w: w.astype(BF16)
    w_in, w_branch_a, w_branch_b, w_out = bf(w_in), bf(w_branch_a), bf(w_branch_b), bf(w_out)
    ffn_w_gate, ffn_w_up, ffn_w_down = bf(ffn_w_gate), bf(ffn_w_up), bf(ffn_w_down)
    merge = lambda w: bf(w).reshape((-1,) + w.shape[2:])
    moe_w_gate, moe_w_up, moe_w_down = merge(moe_w_gate), merge(moe_w_up), merge(moe_w_down)

    xf, xb = _ln_in([x_prompt.reshape(-1, D), x_sample.reshape(-1, D)], ln_in_g, ln_in_b, tl['ln'])
    for l in range(depth):
        lam_init = 0.8 - 0.6 * math.exp(-0.3 * l)
        consts = jnp.full((1, LANES), lam_init, F32)
        proj = _proj(xb, w_in, l, tab, seqs, tl['proj'])

        att, row0 = [], 0
        for n_seq, S in seqs:
            groups = [_attn_a_group(proj, g, n_seq, S, row0) for g in range(len(A_GROUPS))]
            ob = _attn_b(proj, diff_lambda[l], diff_subln_g[l], consts, n_seq, S, row0)
            att.append([o for o, _ in groups] + [lse for _, lse in groups] + [ob])
            row0 += n_seq * S

        moe_layer = l % 2 == 1
        i = l // 2
        wr = None
        if moe_layer:
            wr = jnp.zeros((D, LANES), F32).at[:, :N_EXPERTS].set(moe_router[i])
        res = _post(att, proj, xf, w_branch_a, w_branch_b, w_out, l,
                    ln_mix_g[l], ln_mix_b[l], wr, alpha, tl['post'])
        if moe_layer:
            xf, xb, logits = res
            route = _route(logits, tl['moe_m'])
            y = _moe(xf, route, moe_w_gate, moe_w_up, moe_w_down, i, tl['moe_m'], tl['moe_f'])
            if l == depth - 1:
                n_p = seqs[0][0] * seqs[0][1]
                outs = [_combine(xf, y, ln_ffn_g[l], ln_ffn_b[l], alpha, tl['comb'], r0, n, False)[0]
                        for r0, n in ((0, n_p), (n_p, T - n_p))]
                return (outs[0].reshape(x_prompt.shape), outs[1].reshape(x_sample.shape))
            xf, xb = _combine(xf, y, ln_ffn_g[l], ln_ffn_b[l], alpha, tl['comb'], 0, T, True)
        else:
            xf, xb = res
            xf, xb = _ffn(xb, xf, ffn_w_gate, ffn_w_up, ffn_w_down, i,
                          ln_ffn_g[l], ln_ffn_b[l], alpha, tl['ffn_m'], tl['ffn_f'])

    n_p = seqs[0][0] * seqs[0][1]
    return (xf[:n_p].reshape(x_prompt.shape), xf[n_p:].reshape(x_sample.shape))
```

```python
import functools
import math

import jax
import jax.numpy as jnp
from jax import lax
from jax.experimental import pallas as pl
from jax.experimental.pallas import tpu as pltpu

F32 = jnp.float32
BF16 = jnp.bfloat16

HEAD_DIM = 128
A_GROUPS = ((128, 1), (512, 4), (2048, 16))
A_HEADS_PER_GROUP = 4
A_OUT = A_HEADS_PER_GROUP * HEAD_DIM
A_QKV = len(A_GROUPS) * A_OUT
B_HEADS = 4
B_QK_DIM = 64
B_V_DIM = 2 * B_QK_DIM
B_COLS = B_HEADS * B_V_DIM
ROPE_THETA = 500000.0
ROPE_FRACTION = 4
N_EXPERTS = 8
TOP_K = 2
LN_EPS = 1e-5
MASK_VALUE = -1e30
LOG2_E = math.log2(math.e)
GATE_COL0 = 3 * A_QKV + 3 * B_COLS

LANES = 128
COL_BLOCK = 512
VMEM_LIMIT = 56 * 1024 * 1024


def _cparams(sem):
    return pltpu.CompilerParams(dimension_semantics=sem, vmem_limit_bytes=VMEM_LIMIT)


def _layer_norm(y, g, b):
    mu = jnp.mean(y, axis=-1, keepdims=True)
    yc = y - mu
    var = jnp.mean(yc * yc, axis=-1, keepdims=True)
    return yc * lax.rsqrt(var + LN_EPS) * g + b


def _dot(a, b):
    return jnp.dot(a, b, preferred_element_type=F32)


def _dot_nt(a, b):
    return lax.dot_general(a, b, (((1,), (1,)), ((), ())), preferred_element_type=F32)


def _ln_in_kernel(*refs, tile_starts):
    n_trunks = len(tile_starts)
    g_ref, b_ref, xf_ref, xb_ref = refs[n_trunks:]
    i = pl.program_id(0)
    ends = tile_starts[1:] + (pl.num_programs(0),)
    for t in range(n_trunks):
        @pl.when((i >= tile_starts[t]) & (i < ends[t]))
        def _():
            y = _layer_norm(refs[t][...], g_ref[...], b_ref[...])
            xf_ref[...] = y
            xb_ref[...] = y.astype(BF16)


def _ln_in(xs, g, b, tm):
    D = xs[0].shape[1]
    in_specs, tile_starts, t0 = [], [], 0
    for x in xs:
        n_t = x.shape[0] // tm
        in_specs.append(pl.BlockSpec((tm, D), lambda i, t0=t0, n_t=n_t: (jnp.clip(i - t0, 0, n_t - 1), 0)))
        tile_starts.append(t0)
        t0 += n_t
    row = pl.BlockSpec((tm, D), lambda i: (i, 0))
    vec = pl.BlockSpec((1, D), lambda i: (0, 0))
    return pl.pallas_call(
        functools.partial(_ln_in_kernel, tile_starts=tuple(tile_starts)),
        grid=(t0,),
        in_specs=in_specs + [vec, vec],
        out_specs=[row, row],
        out_shape=[jax.ShapeDtypeStruct((t0 * tm, D), F32), jax.ShapeDtypeStruct((t0 * tm, D), BF16)],
        compiler_params=_cparams(("parallel",)),
    )(*xs, g.reshape(1, D), b.reshape(1, D))


def _rope_tables(s_max):
    pos = jnp.arange(s_max, dtype=F32)[:, None]

    def tables(head, n_rep):
        rot = head // ROPE_FRACTION
        half = rot // 2
        inv_freq = ROPE_THETA ** (-jnp.arange(half, dtype=F32) / half)
        ang = pos * inv_freq[None, :]
        cos, sin = jnp.cos(ang), jnp.sin(ang)
        zero = jnp.zeros((s_max, head - rot), F32)
        zh = jnp.zeros((s_max, half), F32)
        c = jnp.concatenate([cos, cos, jnp.ones((s_max, head - rot), F32)], axis=1)
        s_up = jnp.concatenate([-sin, zh, zero], axis=1)
        s_dn = jnp.concatenate([zh, sin, zero], axis=1)
        return [jnp.tile(t, (1, n_rep)) for t in (c, s_up, s_dn)]

    return jnp.stack(tables(HEAD_DIM, 1) + tables(B_QK_DIM, 2), axis=0)


def _proj_kernel(x_ref, w_ref, tab_ref, o_ref):
    j = pl.program_id(1)
    n_a = A_QKV // COL_BLOCK
    is_a = j < 2 * n_a
    jb = j - 3 * n_a
    is_b = (jb == 0) | (jb == 1)

    def rope(t0, half, scale):
        n_parts = 4
        part = x_ref.shape[0] // n_parts
        parts = [pl.ds(a * part, part) for a in range(n_parts)]
        accs = [_dot(x_ref[rows, :], w_ref[...]) for rows in parts]
        for rows, acc in zip(parts, accs):
            for c in range(COL_BLOCK // LANES):
                sl = slice(c * LANES, (c + 1) * LANES)
                xc = acc[:, sl]
                up = pltpu.roll(xc, LANES - half, 1)
                dn = pltpu.roll(xc, half, 1)
                y = xc * tab_ref[t0, rows, :] + up * tab_ref[t0 + 1, rows, :] + dn * tab_ref[t0 + 2, rows, :]
                o_ref[rows, sl] = (y * scale).astype(BF16)

    @pl.when(is_a)
    def _():
        rope(0, HEAD_DIM // ROPE_FRACTION // 2, jnp.where(j < n_a, HEAD_DIM ** -0.5, 1.0).astype(F32))

    @pl.when(is_b)
    def _():
        rope(3, B_QK_DIM // ROPE_FRACTION // 2, jnp.where(jb == 0, B_QK_DIM ** -0.5 * LOG2_E, 1.0).astype(F32))

    @pl.when(jnp.logical_not(is_a | is_b))
    def _():
        o_ref[...] = _dot(x_ref[...], w_ref[...]).astype(BF16)


def _proj(xb, w, l, tab, seqs, tm):
    T, D = xb.shape
    C = w.shape[2]
    bounds = []
    t0 = 0
    for n_seq, S in seqs:
        assert S % tm == 0
        bounds.append((t0 // tm, S // tm))
        t0 += n_seq * S
    assert t0 == T

    def pos_block(i):
        blk = (i - bounds[0][0]) % bounds[0][1]
        for first, per in bounds[1:]:
            blk = jnp.where(i >= first, (i - first) % per, blk)
        return blk

    return pl.pallas_call(
        _proj_kernel,
        grid=(T // tm, C // COL_BLOCK),
        in_specs=[
            pl.BlockSpec((tm, D), lambda i, j: (i, 0)),
            pl.BlockSpec((None, D, COL_BLOCK), lambda i, j: (l, 0, j)),
            pl.BlockSpec((6, tm, LANES), lambda i, j: (0, pos_block(i), 0)),
        ],
        out_specs=pl.BlockSpec((tm, COL_BLOCK), lambda i, j: (i, j)),
        out_shape=jax.ShapeDtypeStruct((T, C), BF16),
        compiler_params=_cparams(("parallel", "arbitrary")),
    )(xb, w, tab)


A_TILES_IN_FLIGHT = 16


def _attend_tiles(tiles, n_side):
    scores = [_dot_nt(q, k) for q, k, _, _, _ in tiles]
    probs, stats = [], []
    for s, (_, _, _, q0, start) in zip(scores, tiles):
        tq, kw = s.shape
        qpos = q0 + lax.broadcasted_iota(jnp.int32, (tq, 1), 0)
        kpos = start + lax.broadcasted_iota(jnp.int32, (1, kw), 1)
        s = jnp.where(jnp.abs(kpos - qpos) <= n_side, s, MASK_VALUE)
        m = jnp.max(s, axis=-1, keepdims=True)
        p = jnp.exp(s - m)
        probs.append(p.astype(BF16))
        stats.append((m, jnp.sum(p, axis=-1, keepdims=True)))
    outs = [_dot(p, v) for p, (_, _, v, _, _) in zip(probs, tiles)]
    return [(o / den, m + jnp.log(den)) for o, (m, den) in zip(outs, stats)]


def _attn_a_kernel(q_ref, k_ref, v_ref, o_ref, lse_ref, *, L, tq, kw, n_side):
    n_q = L // tq
    group = max(1, A_TILES_IN_FLIGHT // A_HEADS_PER_GROUP)
    assert n_q % group == 0

    def body(it, carry):
        where, tiles = [], []
        for t in range(group):
            q0 = pl.multiple_of((it * group + t) * tq, tq)
            start = pl.multiple_of(jnp.clip(q0 - n_side, 0, L - kw), n_side)
            for h in range(A_HEADS_PER_GROUP):
                sl = slice(h * HEAD_DIM, (h + 1) * HEAD_DIM)
                where.append((q0, sl))
                tiles.append((q_ref[pl.ds(q0, tq), sl], k_ref[pl.ds(start, kw), sl],
                              v_ref[pl.ds(start, kw), sl], q0, start))
        for (q0, sl), (o, lse) in zip(where, _attend_tiles(tiles, n_side)):
            o_ref[pl.ds(q0, tq), sl] = o.astype(BF16)
            lse_ref[pl.ds(q0, tq), sl] = jnp.broadcast_to(lse, (tq, HEAD_DIM))
        return carry

    lax.fori_loop(0, n_q // group, body, 0)


def _attn_a_dil_kernel(q_ref, k_ref, v_ref, o_ref, lse_ref, qf_ref, kf_ref, vf_ref, *, S, d, tq, kw, n_side, classes):
    L = S // d
    qf_ref[...] = q_ref[...].astype(F32)
    kf_ref[...] = k_ref[...].astype(F32)
    vf_ref[...] = v_ref[...].astype(F32)

    def rows(first, n, r):
        return pl.ds(r + first * d, n, stride=d)

    def body(it, carry):
        where, tiles = [], []
        for u in range(classes):
            r = it * classes + u
            for qi in range(L // tq):
                q0 = qi * tq
                start = min(max(q0 - n_side, 0), L - kw)
                where.append(rows(q0, tq, r))
                tiles.append((qf_ref[rows(q0, tq, r), :].astype(BF16), kf_ref[rows(start, kw, r), :].astype(BF16),
                              vf_ref[rows(start, kw, r), :].astype(BF16), q0, start))
        for idx, (o, lse) in zip(where, _attend_tiles(tiles, n_side)):
            o_ref[idx, :] = o
            lse_ref[idx, :] = jnp.broadcast_to(lse, (tq, HEAD_DIM))
        return carry

    lax.fori_loop(0, d // classes, body, 0)


def _attn_a_group(proj, g, n_seq, S, row0):
    window, d = A_GROUPS[g]
    n_side = window // (2 * d)
    L = S // d
    tq = min(128, L)
    kw = min(L, tq + 2 * n_side)
    assert L % tq == 0 and tq % n_side == 0 and (L - kw) % n_side == 0
    n_g = len(A_GROUPS)
    blk0 = row0 // S
    o_dtype = BF16 if d == 1 else F32
    out_shape = [jax.ShapeDtypeStruct((n_seq * S, A_OUT), o_dtype), jax.ShapeDtypeStruct((n_seq * S, A_OUT), F32)]
    if d == 1:
        spec = lambda part: pl.BlockSpec((S, COL_BLOCK), lambda b: (blk0 + b, part * n_g + g))
        out_spec = pl.BlockSpec((S, A_OUT), lambda b: (b, 0))
        return pl.pallas_call(
            functools.partial(_attn_a_kernel, L=L, tq=tq, kw=kw, n_side=n_side),
            grid=(n_seq,),
            in_specs=[spec(0), spec(1), spec(2)],
            out_specs=[out_spec, out_spec],
            out_shape=out_shape,
            compiler_params=_cparams(("parallel",)),
        )(proj, proj, proj)
    heads = n_g * A_HEADS_PER_GROUP
    spec = lambda part: pl.BlockSpec((S, HEAD_DIM), lambda b, h: (blk0 + b, part * heads + g * A_HEADS_PER_GROUP + h))
    out_spec = pl.BlockSpec((S, HEAD_DIM), lambda b, h: (b, h))
    classes = max(1, min(d, A_TILES_IN_FLIGHT // (L // tq)))
    assert d % classes == 0
    return pl.pallas_call(
        functools.partial(_attn_a_dil_kernel, S=S, d=d, tq=tq, kw=kw, n_side=n_side, classes=classes),
        grid=(n_seq, A_HEADS_PER_GROUP),
        in_specs=[spec(0), spec(1), spec(2)],
        out_specs=[out_spec, out_spec],
        out_shape=out_shape,
        scratch_shapes=[pltpu.VMEM((S, HEAD_DIM), F32)] * 3,
        compiler_params=_cparams(("parallel", "parallel")),
    )(proj, proj, proj)


B_ROW_CHUNK = 256
B_SCORE_BYTES = 32 << 20


def _attn_b_kernel(q_ref, k_ref, v_ref, lam_ref, g_ref, c_ref, o_ref):
    k = k_ref[...]
    v = v_ref[...]
    lane = lax.broadcasted_iota(jnp.int32, (1, 2 * B_QK_DIM), 1)
    lam_init = c_ref[0:1, 0:1]
    lv = lam_ref[...]
    lam = (jnp.exp(jnp.sum(lv[0:1] * lv[1:2], axis=-1, keepdims=True))
           - jnp.exp(jnp.sum(lv[2:3] * lv[3:4], axis=-1, keepdims=True)) + lam_init)

    chunks = [pl.ds(a, B_ROW_CHUNK) for a in range(0, q_ref.shape[0], B_ROW_CHUNK)]
    scores = []
    for rows in chunks:
        q = q_ref[rows, :]
        zero = jnp.zeros_like(q)
        scores.append((_dot_nt(jnp.where(lane < B_QK_DIM, q, zero), k),
                       _dot_nt(jnp.where(lane >= B_QK_DIM, q, zero), k)))

    def softmax(s):
        p = jnp.exp2(s - jnp.max(s, axis=-1, keepdims=True))
        return p, jnp.sum(p, axis=-1, keepdims=True)

    for rows, (s1, s2) in zip(chunks, scores):
        p1, den1 = softmax(s1)
        p2, den2 = softmax(s2)
        o = _dot((p1 - p2 * (lam * den1 / den2)).astype(BF16), v) / den1
        o = o * lax.rsqrt(jnp.mean(o * o, axis=-1, keepdims=True) + LN_EPS) * g_ref[...] * (1.0 - lam_init)
        o_ref[rows, :] = o.astype(BF16)


def _attn_b(proj, lam_vecs, subln_g, consts, n_seq, S, row0):
    tq = min(S, B_ROW_CHUNK * max(1, B_SCORE_BYTES // (2 * B_ROW_CHUNK * S * 4)))
    assert S % tq == 0 and tq % B_ROW_CHUNK == 0
    blk0 = row0 // S
    qb0 = row0 // tq
    n_qt = S // tq
    col = lambda part: (3 * A_QKV + part * B_COLS) // LANES
    return pl.pallas_call(
        _attn_b_kernel,
        grid=(n_seq, B_HEADS, n_qt),
        in_specs=[
            pl.BlockSpec((tq, LANES), lambda b, h, i: (qb0 + b * n_qt + i, col(0) + h)),
            pl.BlockSpec((S, LANES), lambda b, h, i: (blk0 + b, col(1) + h)),
            pl.BlockSpec((S, LANES), lambda b, h, i: (blk0 + b, col(2) + h)),
            pl.BlockSpec((4, B_QK_DIM), lambda b, h, i: (0, 0)),
            pl.BlockSpec((1, B_V_DIM), lambda b, h, i: (0, 0)),
            pl.BlockSpec((1, LANES), lambda b, h, i: (0, 0)),
        ],
        out_specs=pl.BlockSpec((tq, LANES), lambda b, h, i: (b * n_qt + i, h)),
        out_shape=jax.ShapeDtypeStruct((n_seq * S, B_COLS), BF16),
        compiler_params=_cparams(("parallel", "parallel", "arbitrary")),
    )(proj, proj, proj, lam_vecs, subln_g.reshape(1, B_V_DIM), consts)


N_ATT = 7


def _post_kernel(*refs, alpha, with_router, tile_starts):
    n_trunks = len(tile_starts)
    att, rest = refs[:N_ATT * n_trunks], refs[N_ATT * n_trunks:]
    ga_ref, gb_ref, x_ref, wa_ref, wb_ref, wo_ref, g_ref, b_ref = rest[:8]
    if with_router:
        wrh_ref, wrl_ref, xf_ref, xb_ref, lg_ref = rest[8:]
    else:
        xf_ref, xb_ref = rest[8:]

    def body(o1_ref, o2_ref, o3_ref, l1_ref, l2_ref, l3_ref, ob_ref):
        half = x_ref.shape[0] // 2
        halves = (pl.ds(0, half), pl.ds(half, half))
        branches = []
        for rows in halves:
            l1, l2, l3 = l1_ref[rows, :], l2_ref[rows, :], l3_ref[rows, :]
            m = jnp.maximum(jnp.maximum(l1, l2), l3)
            e1, e2, e3 = jnp.exp(l1 - m), jnp.exp(l2 - m), jnp.exp(l3 - m)
            oa = (e1 * o1_ref[rows, :].astype(F32) + e2 * o2_ref[rows, :].astype(F32)
                  + e3 * o3_ref[rows, :].astype(F32)) / (e1 + e2 + e3)
            branches.append((_dot(oa.astype(BF16), wa_ref[...]), _dot(ob_ref[rows, :], wb_ref[...])))
        zs = []
        for rows, (ya, yb) in zip(halves, branches):
            mix = (jax.nn.sigmoid(ga_ref[rows, :].astype(F32)) * ya
                   + jax.nn.sigmoid(gb_ref[rows, :].astype(F32)) * yb)
            zs.append(_dot(mix.astype(BF16), wo_ref[...]))
        for rows, z in zip(halves, zs):
            y = _layer_norm(alpha * x_ref[rows, :] + z, g_ref[...], b_ref[...])
            yh = y.astype(BF16)
            xf_ref[rows, :] = y
            xb_ref[rows, :] = yh
            if with_router:
                yl = (y - yh.astype(F32)).astype(BF16)
                lg_ref[rows, :] = _dot(yh, wrh_ref[...]) + (_dot(yl, wrh_ref[...]) + _dot(yh, wrl_ref[...]))

    i = pl.program_id(0)
    ends = tile_starts[1:] + (pl.num_programs(0),)
    for t in range(n_trunks):
        @pl.when((i >= tile_starts[t]) & (i < ends[t]))
        def _():
            body(*att[N_ATT * t:N_ATT * (t + 1)])


def _post(att, proj, x, wa, wb, wo, l, g, b, wr, alpha, tm):
    T, D = x.shape
    assert GATE_COL0 % D == 0
    gcol = GATE_COL0 // D
    row = lambda w: pl.BlockSpec((tm, w), lambda i: (i, 0))
    full = lambda a: pl.BlockSpec(a.shape, lambda i: (0,) * a.ndim)
    layer = lambda a: pl.BlockSpec((None,) + a.shape[1:], lambda i: (l, 0, 0))
    g2, b2 = g.reshape(1, D), b.reshape(1, D)
    in_specs, args, tile_starts, t0 = [], [], [], 0
    for arrays in att:
        n_t = arrays[0].shape[0] // tm
        for a in arrays:
            in_specs.append(pl.BlockSpec((tm, a.shape[1]), lambda i, t0=t0, n_t=n_t: (jnp.clip(i - t0, 0, n_t - 1), 0)))
            args.append(a)
        tile_starts.append(t0)
        t0 += n_t
    assert t0 == T // tm
    in_specs += [pl.BlockSpec((tm, D), lambda i: (i, gcol)),
                 pl.BlockSpec((tm, D), lambda i: (i, gcol + 1)),
                 row(D), layer(wa), layer(wb), layer(wo), full(g2), full(b2)]
    args += [proj, proj, x, wa, wb, wo, g2, b2]
    out_specs = [row(D), row(D)]
    out_shape = [jax.ShapeDtypeStruct((T, D), F32), jax.ShapeDtypeStruct((T, D), BF16)]
    if wr is not None:
        wr_hi = wr.astype(BF16)
        wr_lo = (wr - wr_hi.astype(F32)).astype(BF16)
        in_specs += [full(wr_hi), full(wr_lo)]
        args += [wr_hi, wr_lo]
        out_specs.append(row(LANES))
        out_shape.append(jax.ShapeDtypeStruct((T, LANES), F32))
    return pl.pallas_call(
        functools.partial(_post_kernel, alpha=alpha, with_router=wr is not None, tile_starts=tuple(tile_starts)),
        grid=(T // tm,),
        in_specs=in_specs,
        out_specs=out_specs,
        out_shape=out_shape,
        compiler_params=_cparams(("parallel",)),
    )(*args)


def _ffn_kernel(xb_ref, x_ref, wg_ref, wu_ref, wd_ref, g_ref, b_ref, xf_ref, xo_ref, acc_ref, *, alpha):
    f = pl.program_id(1)

    @pl.when(f == 0)
    def _():
        acc_ref[...] = jnp.zeros_like(acc_ref)

    xb = xb_ref[...]
    h = jax.nn.silu(_dot(xb, wg_ref[...])) * _dot(xb, wu_ref[...])
    acc_ref[...] += _dot(h.astype(BF16), wd_ref[...])

    @pl.when(f == pl.num_programs(1) - 1)
    def _():
        y = _layer_norm(alpha * x_ref[...] + acc_ref[...], g_ref[...], b_ref[...])
        xf_ref[...] = y
        xo_ref[...] = y.astype(BF16)


def _ffn(xb, x, wg, wu, wd, li, g, b, alpha, tm, tf):
    T, D = x.shape
    F = wg.shape[2]
    row = pl.BlockSpec((tm, D), lambda i, f: (i, 0))
    vec = pl.BlockSpec((1, D), lambda i, f: (0, 0))
    return pl.pallas_call(
        functools.partial(_ffn_kernel, alpha=alpha),
        grid=(T // tm, F // tf),
        in_specs=[row, row,
                  pl.BlockSpec((None, D, tf), lambda i, f: (li, 0, f)),
                  pl.BlockSpec((None, D, tf), lambda i, f: (li, 0, f)),
                  pl.BlockSpec((None, tf, D), lambda i, f: (li, f, 0)),
                  vec, vec],
        out_specs=[row, row],
        out_shape=[jax.ShapeDtypeStruct((T, D), F32), jax.ShapeDtypeStruct((T, D), BF16)],
        scratch_shapes=[pltpu.VMEM((tm, D), F32)],
        compiler_params=_cparams(("parallel", "arbitrary")),
    )(xb, x, wg, wu, wd, g.reshape(1, D), b.reshape(1, D))


def _route(logits, tm):
    T = logits.shape[0]
    top_logits, top_idx = lax.top_k(logits[:, :N_EXPERTS], TOP_K)
    gates = jax.nn.softmax(top_logits, axis=-1)
    n_assign = T * TOP_K
    flat_e = top_idx.reshape(n_assign).astype(jnp.int32)
    onehot = (flat_e[:, None] == jnp.arange(N_EXPERTS, dtype=jnp.int32)[None, :]).astype(jnp.int32)
    csum = jnp.cumsum(onehot, axis=0)
    counts = csum[-1]
    rank = jnp.sum((csum - onehot) * onehot, axis=1)
    padded = (counts + tm - 1) // tm * tm
    pad_end = jnp.cumsum(padded)
    pad_start = pad_end - padded
    dest = pad_start[flat_e] + rank
    n_blocks = -(-(n_assign + N_EXPERTS * (tm - 1)) // tm)
    n_rows = n_blocks * tm
    row_assign = jnp.full((n_rows,), -1, jnp.int32).at[dest].set(jnp.arange(n_assign, dtype=jnp.int32))
    used = row_assign >= 0
    row_src = jnp.where(used, row_assign // TOP_K, 0)
    spare = TOP_K * T + jnp.arange(n_rows, dtype=jnp.int32) % tm
    row_dst = jnp.where(used, (row_assign % TOP_K) * T + row_assign // TOP_K, spare)
    row_gate = jnp.where(used, gates.reshape(n_assign)[jnp.maximum(row_assign, 0)], 0.0)
    block_start = jnp.arange(n_blocks, dtype=jnp.int32) * tm
    block_expert = jnp.minimum(jnp.searchsorted(pad_end, block_start, side='right'), N_EXPERTS - 1).astype(jnp.int32)
    n_valid = (pad_end[-1] // tm).astype(jnp.int32).reshape(1)
    return block_expert, n_valid, row_src, row_dst, jnp.broadcast_to(row_gate[:, None], (n_rows, LANES))


def _moe_kernel(be_ref, nv_ref, src_ref, dst_ref, x_hbm, gate_ref, wg_ref, wu_ref, wd_ref, y_hbm,
                xg_ref, xb_ref, acc_ref, yo_ref, sem_in, sem_out, *, tm, chunk):
    i = pl.program_id(0)
    f = pl.program_id(1)
    n_f = pl.num_programs(1)
    n_blk = pl.num_programs(0)
    n_live = nv_ref[0]
    live = i < n_live
    slot = i % 2
    n_chunks = tm // chunk

    def gather_chunk(blk, buf, c):
        for r in range(c * chunk, (c + 1) * chunk):
            tok = src_ref[blk * tm + r]
            pltpu.make_async_copy(x_hbm.at[pl.ds(tok, 1)], xg_ref.at[buf, pl.ds(r, 1)], sem_in.at[buf]).start()

    def scatter_chunk(blk, c):
        for r in range(c * chunk, (c + 1) * chunk):
            dst = dst_ref[blk * tm + r]
            pltpu.make_async_copy(yo_ref.at[pl.ds(r, 1)], y_hbm.at[pl.ds(dst, 1)], sem_out).start()

    def wait_scatter():
        pltpu.make_async_copy(yo_ref, yo_ref, sem_out).wait()

    @pl.when((i == 0) & (f == 0))
    def _():
        for c in range(n_chunks):
            gather_chunk(0, 0, c)
        yo_ref[...] = jnp.zeros_like(yo_ref)
        spare = pltpu.make_async_copy(yo_ref, y_hbm.at[pl.ds(y_hbm.shape[0] - tm, tm)], sem_out)
        spare.start()
        spare.wait()

    @pl.when(live & (f == 0))
    def _():
        pltpu.make_async_copy(xg_ref.at[slot], xg_ref.at[slot], sem_in.at[slot]).wait()
        xb_ref[...] = xg_ref[slot].astype(BF16)
        acc_ref[...] = jnp.zeros_like(acc_ref)

    for c in range(n_chunks):
        @pl.when((f == c) & (i + 1 < n_live))
        def _():
            gather_chunk(i + 1, 1 - slot, c)

        @pl.when((f == c) & (i >= 1) & (i <= n_live))
        def _():
            scatter_chunk(i - 1, c)

    @pl.when(live)
    def _():
        xb = xb_ref[...]
        h = jax.nn.silu(_dot(xb, wg_ref[0])) * _dot(xb, wu_ref[0])
        acc_ref[...] += _dot(h.astype(BF16), wd_ref[0])

    @pl.when((f == n_f - 1) & (i >= 1) & (i <= n_live))
    def _():
        wait_scatter()

    @pl.when(live & (f == n_f - 1))
    def _():
        gate = gate_ref[...]
        for c in range(acc_ref.shape[1] // LANES):
            sl = slice(c * LANES, (c + 1) * LANES)
            yo_ref[:, sl] = acc_ref[:, sl] * gate

        @pl.when(i == n_blk - 1)
        def _():
            for c in range(n_chunks):
                scatter_chunk(i, c)
            wait_scatter()


def _moe(x, route, wg, wu, wd, li, tm, tf):
    T, D = x.shape
    F = wg.shape[2]
    nf = F // tf
    block_expert, n_valid, row_src, row_dst, row_gate = route
    n_blocks = block_expert.shape[0]
    chunk = next(c for c in range(8, tm + 1, 8) if tm % c == 0 and tm // c <= nf)

    def fidx(i, f, nv):
        return jnp.where(i < nv[0], f, nf - 1)

    grid_spec = pltpu.PrefetchScalarGridSpec(
        num_scalar_prefetch=4,
        grid=(n_blocks, nf),
        in_specs=[
            pl.BlockSpec(memory_space=pl.ANY),
            pl.BlockSpec((tm, LANES), lambda i, f, be, nv, rs, rd: (i, 0)),
            pl.BlockSpec((1, D, tf), lambda i, f, be, nv, rs, rd: (li * N_EXPERTS + be[i], 0, fidx(i, f, nv))),
            pl.BlockSpec((1, D, tf), lambda i, f, be, nv, rs, rd: (li * N_EXPERTS + be[i], 0, fidx(i, f, nv))),
            pl.BlockSpec((1, tf, D), lambda i, f, be, nv, rs, rd: (li * N_EXPERTS + be[i], fidx(i, f, nv), 0)),
        ],
        out_specs=pl.BlockSpec(memory_space=pl.ANY),
        scratch_shapes=[pltpu.VMEM((2, tm, D), F32), pltpu.VMEM((tm, D), BF16), pltpu.VMEM((tm, D), F32),
                        pltpu.VMEM((tm, D), F32), pltpu.SemaphoreType.DMA((2,)), pltpu.SemaphoreType.DMA(())],
    )
    y = pl.pallas_call(
        functools.partial(_moe_kernel, tm=tm, chunk=chunk),
        grid_spec=grid_spec,
        out_shape=jax.ShapeDtypeStruct((TOP_K * T + tm, D), F32),
        compiler_params=_cparams(("arbitrary", "arbitrary")),
    )(block_expert, n_valid, row_src, row_dst, x, row_gate, wg, wu, wd)
    return y


def _combine_kernel(x_ref, y0_ref, y1_ref, g_ref, b_ref, xf_ref, *rest, alpha):
    y = _layer_norm(alpha * x_ref[...] + (y0_ref[...] + y1_ref[...]), g_ref[...], b_ref[...])
    xf_ref[...] = y
    for xb_ref in rest:
        xb_ref[...] = y.astype(BF16)


def _combine(x, y, g, b, alpha, tm, row0, n_rows, with_bf16):
    T, D = x.shape
    n = T // tm
    i0 = row0 // tm
    row = pl.BlockSpec((tm, D), lambda i: (i, 0))
    vec = pl.BlockSpec((1, D), lambda i: (0, 0))
    out_specs = [row, row] if with_bf16 else [row]
    out_shape = [jax.ShapeDtypeStruct((n_rows, D), F32)]
    if with_bf16:
        out_shape.append(jax.ShapeDtypeStruct((n_rows, D), BF16))
    return pl.pallas_call(
        functools.partial(_combine_kernel, alpha=alpha),
        grid=(n_rows // tm,),
        in_specs=[pl.BlockSpec((tm, D), lambda i: (i0 + i, 0)),
                  pl.BlockSpec((tm, D), lambda i: (i0 + i, 0)),
                  pl.BlockSpec((tm, D), lambda i: (n + i0 + i, 0)), vec, vec],
        out_specs=out_specs,
        out_shape=out_shape,
        compiler_params=_cparams(("parallel",)),
    )(x, y, y, g.reshape(1, D), b.reshape(1, D))


def _tiles(D):
    big = D >= 2048
    return dict(ln=512, proj=2048, post=256 if big else 512, ffn_m=512, ffn_f=512,
                moe_m=512, moe_f=1024, comb=512)


def kernel(x_prompt, x_sample, ln_in_g, ln_in_b, w_in, w_branch_a, w_branch_b, w_out, diff_lambda, diff_subln_g, ln_mix_g, ln_mix_b, ln_ffn_g, ln_ffn_b, ffn_w_gate, ffn_w_up, ffn_w_down, moe_router, moe_w_gate, moe_w_up, moe_w_down):
    depth = w_in.shape[0]
    D = x_prompt.shape[-1]
    alpha = (2 * depth) ** 0.25
    tl = _tiles(D)
    seqs = [(x_prompt.shape[0], x_prompt.shape[1]), (x_sample.shape[0], x_sample.shape[1])]
    T = sum(n_seq * S for n_seq, S in seqs)
    tab = _rope_tables(max(S for _, S in seqs))
    bf = lambda w: w.astype(BF16)
    w_in, w_branch_a, w_branch_b, w_out = bf(w_in), bf(w_branch_a), bf(w_branch_b), bf(w_out)
    ffn_w_gate, ffn_w_up, ffn_w_down = bf(ffn_w_gate), bf(ffn_w_up), bf(ffn_w_down)
    merge = lambda w: bf(w).reshape((-1,) + w.shape[2:])
    moe_w_gate, moe_w_up, moe_w_down = merge(moe_w_gate), merge(moe_w_up), merge(moe_w_down)

    xf, xb = _ln_in([x_prompt.reshape(-1, D), x_sample.reshape(-1, D)], ln_in_g, ln_in_b, tl['ln'])
    for l in range(depth):
        lam_init = 0.8 - 0.6 * math.exp(-0.3 * l)
        consts = jnp.full((1, LANES), lam_init, F32)
        proj = _proj(xb, w_in, l, tab, seqs, tl['proj'])

        att, row0 = [], 0
        for n_seq, S in seqs:
            groups = [_attn_a_group(proj, g, n_seq, S, row0) for g in range(len(A_GROUPS))]
            ob = _attn_b(proj, diff_lambda[l], diff_subln_g[l], consts, n_seq, S, row0)
            att.append([o for o, _ in groups] + [lse for _, lse in groups] + [ob])
            row0 += n_seq * S

        moe_layer = l % 2 == 1
        i = l // 2
        wr = None
        if moe_layer:
            wr = jnp.zeros((D, LANES), F32).at[:, :N_EXPERTS].set(moe_router[i])
        res = _post(att, proj, xf, w_branch_a, w_branch_b, w_out, l,
                    ln_mix_g[l], ln_mix_b[l], wr, alpha, tl['post'])
        if moe_layer:
            xf, xb, logits = res
            route = _route(logits, tl['moe_m'])
            y = _moe(xf, route, moe_w_gate, moe_w_up, moe_w_down, i, tl['moe_m'], tl['moe_f'])
            if l == depth - 1:
                n_p = seqs[0][0] * seqs[0][1]
                outs = [_combine(xf, y, ln_ffn_g[l], ln_ffn_b[l], alpha, tl['comb'], r0, n, False)[0]
                        for r0, n in ((0, n_p), (n_p, T - n_p))]
                return (outs[0].reshape(x_prompt.shape), outs[1].reshape(x_sample.shape))
            xf, xb = _combine(xf, y, ln_ffn_g[l], ln_ffn_b[l], alpha, tl['comb'], 0, T, True)
        else:
            xf, xb = res
            xf, xb = _ffn(xb, xf, ffn_w_gate, ffn_w_up, ffn_w_down, i,
                          ln_ffn_g[l], ln_ffn_b[l], alpha, tl['ffn_m'], tl['ffn_f'])

    n_p = seqs[0][0] * seqs[0][1]
    return (xf[:n_p].reshape(x_prompt.shape), xf[n_p:].reshape(x_sample.shape))
```

```python
import functools
import math

import jax
import jax.numpy as jnp
from jax import lax
from jax.experimental import pallas as pl
from jax.experimental.pallas import tpu as pltpu

F32 = jnp.float32
BF16 = jnp.bfloat16

HEAD_DIM = 128
A_GROUPS = ((128, 1), (512, 4), (2048, 16))
A_HEADS_PER_GROUP = 4
A_OUT = A_HEADS_PER_GROUP * HEAD_DIM
A_QKV = len(A_GROUPS) * A_OUT
B_HEADS = 4
B_QK_DIM = 64
B_V_DIM = 2 * B_QK_DIM
B_COLS = B_HEADS * B_V_DIM
ROPE_THETA = 500000.0
ROPE_FRACTION = 4
N_EXPERTS = 8
TOP_K = 2
LN_EPS = 1e-5
MASK_VALUE = -1e30
LOG2_E = math.log2(math.e)
GATE_COL0 = 3 * A_QKV + 3 * B_COLS

LANES = 128
COL_BLOCK = 512
VMEM_LIMIT = 56 * 1024 * 1024


def _cparams(sem):
    return pltpu.CompilerParams(dimension_semantics=sem, vmem_limit_bytes=VMEM_LIMIT)


def _layer_norm(y, g, b):
    mu = jnp.mean(y, axis=-1, keepdims=True)
    yc = y - mu
    var = jnp.mean(yc * yc, axis=-1, keepdims=True)
    return yc * lax.rsqrt(var + LN_EPS) * g + b


def _dot(a, b):
    return jnp.dot(a, b, preferred_element_type=F32)


def _dot_nt(a, b):
    return lax.dot_general(a, b, (((1,), (1,)), ((), ())), preferred_element_type=F32)


def _ln_in_kernel(*refs, tile_starts):
    n_trunks = len(tile_starts)
    g_ref, b_ref, xf_ref, xb_ref = refs[n_trunks:]
    i = pl.program_id(0)
    ends = tile_starts[1:] + (pl.num_programs(0),)
    for t in range(n_trunks):
        @pl.when((i >= tile_starts[t]) & (i < ends[t]))
        def _():
            y = _layer_norm(refs[t][...], g_ref[...], b_ref[...])
            xf_ref[...] = y
            xb_ref[...] = y.astype(BF16)


def _ln_in(xs, g, b, tm):
    D = xs[0].shape[1]
    in_specs, tile_starts, t0 = [], [], 0
    for x in xs:
        n_t = x.shape[0] // tm
        in_specs.append(pl.BlockSpec((tm, D), lambda i, t0=t0, n_t=n_t: (jnp.clip(i - t0, 0, n_t - 1), 0)))
        tile_starts.append(t0)
        t0 += n_t
    row = pl.BlockSpec((tm, D), lambda i: (i, 0))
    vec = pl.BlockSpec((1, D), lambda i: (0, 0))
    return pl.pallas_call(
        functools.partial(_ln_in_kernel, tile_starts=tuple(tile_starts)),
        grid=(t0,),
        in_specs=in_specs + [vec, vec],
        out_specs=[row, row],
        out_shape=[jax.ShapeDtypeStruct((t0 * tm, D), F32), jax.ShapeDtypeStruct((t0 * tm, D), BF16)],
        compiler_params=_cparams(("parallel",)),
    )(*xs, g.reshape(1, D), b.reshape(1, D))


def _rope_tables(s_max):
    pos = jnp.arange(s_max, dtype=F32)[:, None]

    def tables(head, n_rep):
        rot = head // ROPE_FRACTION
        half = rot // 2
        inv_freq = ROPE_THETA ** (-jnp.arange(half, dtype=F32) / half)
        ang = pos * inv_freq[None, :]
        cos, sin = jnp.cos(ang), jnp.sin(ang)
        zero = jnp.zeros((s_max, head - rot), F32)
        zh = jnp.zeros((s_max, half), F32)
        c = jnp.concatenate([cos, cos, jnp.ones((s_max, head - rot), F32)], axis=1)
        s_up = jnp.concatenate([-sin, zh, zero], axis=1)
        s_dn = jnp.concatenate([zh, sin, zero], axis=1)
        return [jnp.tile(t, (1, n_rep)) for t in (c, s_up, s_dn)]

    return jnp.stack(tables(HEAD_DIM, 1) + tables(B_QK_DIM, 2), axis=0)


def _proj_kernel(x_ref, w_ref, tab_ref, o_ref):
    j = pl.program_id(1)
    n_a = A_QKV // COL_BLOCK
    is_a = j < 2 * n_a
    jb = j - 3 * n_a
    is_b = (jb == 0) | (jb == 1)

    def rope(t0, half, scale):
        n_parts = 4
        part = x_ref.shape[0] // n_parts
        parts = [pl.ds(a * part, part) for a in range(n_parts)]
        accs = [_dot(x_ref[rows, :], w_ref[...]) for rows in parts]
        for rows, acc in zip(parts, accs):
            for c in range(COL_BLOCK // LANES):
                sl = slice(c * LANES, (c + 1) * LANES)
                xc = acc[:, sl]
                up = pltpu.roll(xc, LANES - half, 1)
                dn = pltpu.roll(xc, half, 1)
                y = xc * tab_ref[t0, rows, :] + up * tab_ref[t0 + 1, rows, :] + dn * tab_ref[t0 + 2, rows, :]
                o_ref[rows, sl] = (y * scale).astype(BF16)

    @pl.when(is_a)
    def _():
        rope(0, HEAD_DIM // ROPE_FRACTION // 2, jnp.where(j < n_a, HEAD_DIM ** -0.5, 1.0).astype(F32))

    @pl.when(is_b)
    def _():
        rope(3, B_QK_DIM // ROPE_FRACTION // 2, jnp.where(jb == 0, B_QK_DIM ** -0.5 * LOG2_E, 1.0).astype(F32))

    @pl.when(jnp.logical_not(is_a | is_b))
    def _():
        o_ref[...] = _dot(x_ref[...], w_ref[...]).astype(BF16)


def _proj(xb, w, l, tab, seqs, tm):
    T, D = xb.shape
    C = w.shape[2]
    bounds = []
    t0 = 0
    for n_seq, S in seqs:
        assert S % tm == 0
        bounds.append((t0 // tm, S // tm))
        t0 += n_seq * S
    assert t0 == T

    def pos_block(i):
        blk = (i - bounds[0][0]) % bounds[0][1]
        for first, per in bounds[1:]:
            blk = jnp.where(i >= first, (i - first) % per, blk)
        return blk

    return pl.pallas_call(
        _proj_kernel,
        grid=(T // tm, C // COL_BLOCK),
        in_specs=[
            pl.BlockSpec((tm, D), lambda i, j: (i, 0)),
            pl.BlockSpec((None, D, COL_BLOCK), lambda i, j: (l, 0, j)),
            pl.BlockSpec((6, tm, LANES), lambda i, j: (0, pos_block(i), 0)),
        ],
        out_specs=pl.BlockSpec((tm, COL_BLOCK), lambda i, j: (i, j)),
        out_shape=jax.ShapeDtypeStruct((T, C), BF16),
        compiler_params=_cparams(("parallel", "arbitrary")),
    )(xb, w, tab)


A_TILES_IN_FLIGHT = 16


def _attend_tiles(tiles, n_side):
    scores = [_dot_nt(q, k) for q, k, _, _, _ in tiles]
    probs, stats = [], []
    for s, (_, _, _, q0, start) in zip(scores, tiles):
        tq, kw = s.shape
        qpos = q0 + lax.broadcasted_iota(jnp.int32, (tq, 1), 0)
        kpos = start + lax.broadcasted_iota(jnp.int32, (1, kw), 1)
        s = jnp.where(jnp.abs(kpos - qpos) <= n_side, s, MASK_VALUE)
        m = jnp.max(s, axis=-1, keepdims=True)
        p = jnp.exp(s - m)
        probs.append(p.astype(BF16))
        stats.append((m, jnp.sum(p, axis=-1, keepdims=True)))
    outs = [_dot(p, v) for p, (_, _, v, _, _) in zip(probs, tiles)]
    return [(o / den, m + jnp.log(den)) for o, (m, den) in zip(outs, stats)]


def _attn_a_kernel(q_ref, k_ref, v_ref, o_ref, lse_ref, *, L, tq, kw, n_side):
    n_q = L // tq
    group = max(1, A_TILES_IN_FLIGHT // A_HEADS_PER_GROUP)
    assert n_q % group == 0

    def body(it, carry):
        where, tiles = [], []
        for t in range(group):
            q0 = pl.multiple_of((it * group + t) * tq, tq)
            start = pl.multiple_of(jnp.clip(q0 - n_side, 0, L - kw), n_side)
            for h in range(A_HEADS_PER_GROUP):
                sl = slice(h * HEAD_DIM, (h + 1) * HEAD_DIM)
                where.append((q0, sl))
                tiles.append((q_ref[pl.ds(q0, tq), sl], k_ref[pl.ds(start, kw), sl],
                              v_ref[pl.ds(start, kw), sl], q0, start))
        for (q0, sl), (o, lse) in zip(where, _attend_tiles(tiles, n_side)):
            o_ref[pl.ds(q0, tq), sl] = o.astype(BF16)
            lse_ref[pl.ds(q0, tq), sl] = jnp.broadcast_to(lse, (tq, HEAD_DIM))
        return carry

    lax.fori_loop(0, n_q // group, body, 0)


def _attn_a_dil_kernel(q_ref, k_ref, v_ref, o_ref, lse_ref, qf_ref, kf_ref, vf_ref, *, S, d, tq, kw, n_side, classes):
    L = S // d
    qf_ref[...] = q_ref[...].astype(F32)
    kf_ref[...] = k_ref[...].astype(F32)
    vf_ref[...] = v_ref[...].astype(F32)

    def rows(first, n, r):
        return pl.ds(r + first * d, n, stride=d)

    def body(it, carry):
        where, tiles = [], []
        for u in range(classes):
            r = it * classes + u
            for qi in range(L // tq):
                q0 = qi * tq
                start = min(max(q0 - n_side, 0), L - kw)
                where.append(rows(q0, tq, r))
                tiles.append((qf_ref[rows(q0, tq, r), :].astype(BF16), kf_ref[rows(start, kw, r), :].astype(BF16),
                              vf_ref[rows(start, kw, r), :].astype(BF16), q0, start))
        for idx, (o, lse) in zip(where, _attend_tiles(tiles, n_side)):
            o_ref[idx, :] = o
            lse_ref[idx, :] = jnp.broadcast_to(lse, (tq, HEAD_DIM))
        return carry

    lax.fori_loop(0, d // classes, body, 0)


def _attn_a_group(proj, g, n_seq, S, row0):
    window, d = A_GROUPS[g]
    n_side = window // (2 * d)
    L = S // d
    tq = min(128, L)
    kw = min(L, tq + 2 * n_side)
    assert L % tq == 0 and tq % n_side == 0 and (L - kw) % n_side == 0
    n_g = len(A_GROUPS)
    blk0 = row0 // S
    o_dtype = BF16 if d == 1 else F32
    out_shape = [jax.ShapeDtypeStruct((n_seq * S, A_OUT), o_dtype), jax.ShapeDtypeStruct((n_seq * S, A_OUT), F32)]
    if d == 1:
        spec = lambda part: pl.BlockSpec((S, COL_BLOCK), lambda b: (blk0 + b, part * n_g + g))
        out_spec = pl.BlockSpec((S, A_OUT), lambda b: (b, 0))
        return pl.pallas_call(
            functools.partial(_attn_a_kernel, L=L, tq=tq, kw=kw, n_side=n_side),
            grid=(n_seq,),
            in_specs=[spec(0), spec(1), spec(2)],
            out_specs=[out_spec, out_spec],
            out_shape=out_shape,
            compiler_params=_cparams(("parallel",)),
        )(proj, proj, proj)
    heads = n_g * A_HEADS_PER_GROUP
    spec = lambda part: pl.BlockSpec((S, HEAD_DIM), lambda b, h: (blk0 + b, part * heads + g * A_HEADS_PER_GROUP + h))
    out_spec = pl.BlockSpec((S, HEAD_DIM), lambda b, h: (b, h))
    classes = max(1, min(d, A_TILES_IN_FLIGHT // (L // tq)))
    assert d % classes == 0
    return pl.pallas_call(
        functools.partial(_attn_a_dil_kernel, S=S, d=d, tq=tq, kw=kw, n_side=n_side, classes=classes),
        grid=(n_seq, A_HEADS_PER_GROUP),
        in_specs=[spec(0), spec(1), spec(2)],
        out_specs=[out_spec, out_spec],
        out_shape=out_shape,
        scratch_shapes=[pltpu.VMEM((S, HEAD_DIM), F32)] * 3,
        compiler_params=_cparams(("parallel", "parallel")),
    )(proj, proj, proj)


B_ROW_CHUNK = 256
B_SCORE_BYTES = 32 << 20


def _attn_b_kernel(q_ref, k_ref, v_ref, lam_ref, g_ref, c_ref, o_ref):
    k = k_ref[...]
    v = v_ref[...]
    lane = lax.broadcasted_iota(jnp.int32, (1, 2 * B_QK_DIM), 1)
    lam_init = c_ref[0:1, 0:1]
    lv = lam_ref[...]
    lam = (jnp.exp(jnp.sum(lv[0:1] * lv[1:2], axis=-1, keepdims=True))
           - jnp.exp(jnp.sum(lv[2:3] * lv[3:4], axis=-1, keepdims=True)) + lam_init)

    chunks = [pl.ds(a, B_ROW_CHUNK) for a in range(0, q_ref.shape[0], B_ROW_CHUNK)]
    scores = []
    for rows in chunks:
        q = q_ref[rows, :]
        zero = jnp.zeros_like(q)
        scores.append((_dot_nt(jnp.where(lane < B_QK_DIM, q, zero), k),
                       _dot_nt(jnp.where(lane >= B_QK_DIM, q, zero), k)))

    def softmax(s):
        p = jnp.exp2(s - jnp.max(s, axis=-1, keepdims=True))
        return p, jnp.sum(p, axis=-1, keepdims=True)

    for rows, (s1, s2) in zip(chunks, scores):
        p1, den1 = softmax(s1)
        p2, den2 = softmax(s2)
        o = _dot((p1 - p2 * (lam * den1 / den2)).astype(BF16), v) / den1
        o = o * lax.rsqrt(jnp.mean(o * o, axis=-1, keepdims=True) + LN_EPS) * g_ref[...] * (1.0 - lam_init)
        o_ref[rows, :] = o.astype(BF16)


def _attn_b(proj, lam_vecs, subln_g, consts, n_seq, S, row0):
    tq = min(S, B_ROW_CHUNK * max(1, B_SCORE_BYTES // (2 * B_ROW_CHUNK * S * 4)))
    assert S % tq == 0 and tq % B_ROW_CHUNK == 0
    blk0 = row0 // S
    qb0 = row0 // tq
    n_qt = S // tq
    col = lambda part: (3 * A_QKV + part * B_COLS) // LANES
    return pl.pallas_call(
        _attn_b_kernel,
        grid=(n_seq, B_HEADS, n_qt),
        in_specs=[
            pl.BlockSpec((tq, LANES), lambda b, h, i: (qb0 + b * n_qt + i, col(0) + h)),
            pl.BlockSpec((S, LANES), lambda b, h, i: (blk0 + b, col(1) + h)),
            pl.BlockSpec((S, LANES), lambda b, h, i: (blk0 + b, col(2) + h)),
            pl.BlockSpec((4, B_QK_DIM), lambda b, h, i: (0, 0)),
            pl.BlockSpec((1, B_V_DIM), lambda b, h, i: (0, 0)),
            pl.BlockSpec((1, LANES), lambda b, h, i: (0, 0)),
        ],
        out_specs=pl.BlockSpec((tq, LANES), lambda b, h, i: (b * n_qt + i, h)),
        out_shape=jax.ShapeDtypeStruct((n_seq * S, B_COLS), BF16),
        compiler_params=_cparams(("parallel", "parallel", "arbitrary")),
    )(proj, proj, proj, lam_vecs, subln_g.reshape(1, B_V_DIM), consts)


N_ATT = 7


def _post_kernel(*refs, alpha, with_router, tile_starts):
    n_trunks = len(tile_starts)
    att, rest = refs[:N_ATT * n_trunks], refs[N_ATT * n_trunks:]
    ga_ref, gb_ref, x_ref, wa_ref, wb_ref, wo_ref, g_ref, b_ref = rest[:8]
    if with_router:
        wrh_ref, wrl_ref, xf_ref, xb_ref, lg_ref = rest[8:]
    else:
        xf_ref, xb_ref = rest[8:]

    def body(o1_ref, o2_ref, o3_ref, l1_ref, l2_ref, l3_ref, ob_ref):
        half = x_ref.shape[0] // 2
        halves = (pl.ds(0, half), pl.ds(half, half))
        branches = []
        for rows in halves:
            l1, l2, l3 = l1_ref[rows, :], l2_ref[rows, :], l3_ref[rows, :]
            m = jnp.maximum(jnp.maximum(l1, l2), l3)
            e1, e2, e3 = jnp.exp(l1 - m), jnp.exp(l2 - m), jnp.exp(l3 - m)
            oa = (e1 * o1_ref[rows, :].astype(F32) + e2 * o2_ref[rows, :].astype(F32)
                  + e3 * o3_ref[rows, :].astype(F32)) / (e1 + e2 + e3)
            branches.append((_dot(oa.astype(BF16), wa_ref[...]), _dot(ob_ref[rows, :], wb_ref[...])))
        zs = []
        for rows, (ya, yb) in zip(halves, branches):
            mix = (jax.nn.sigmoid(ga_ref[rows, :].astype(F32)) * ya
                   + jax.nn.sigmoid(gb_ref[rows, :].astype(F32)) * yb)
            zs.append(_dot(mix.astype(BF16), wo_ref[...]))
        for rows, z in zip(halves, zs):
            y = _layer_norm(alpha * x_ref[rows, :] + z, g_ref[...], b_ref[...])
            yh = y.astype(BF16)
            xf_ref[rows, :] = y
            xb_ref[rows, :] = yh
            if with_router:
                yl = (y - yh.astype(F32)).astype(BF16)
                lg_ref[rows, :] = _dot(yh, wrh_ref[...]) + (_dot(yl, wrh_ref[...]) + _dot(yh, wrl_ref[...]))

    i = pl.program_id(0)
    ends = tile_starts[1:] + (pl.num_programs(0),)
    for t in range(n_trunks):
        @pl.when((i >= tile_starts[t]) & (i < ends[t]))
        def _():
            body(*att[N_ATT * t:N_ATT * (t + 1)])


def _post(att, proj, x, wa, wb, wo, l, g, b, wr, alpha, tm):
    T, D = x.shape
    assert GATE_COL0 % D == 0
    gcol = GATE_COL0 // D
    row = lambda w: pl.BlockSpec((tm, w), lambda i: (i, 0))
    full = lambda a: pl.BlockSpec(a.shape, lambda i: (0,) * a.ndim)
    layer = lambda a: pl.BlockSpec((None,) + a.shape[1:], lambda i: (l, 0, 0))
    g2, b2 = g.reshape(1, D), b.reshape(1, D)
    in_specs, args, tile_starts, t0 = [], [], [], 0
    for arrays in att:
        n_t = arrays[0].shape[0] // tm
        for a in arrays:
            in_specs.append(pl.BlockSpec((tm, a.shape[1]), lambda i, t0=t0, n_t=n_t: (jnp.clip(i - t0, 0, n_t - 1), 0)))
            args.append(a)
        tile_starts.append(t0)
        t0 += n_t
    assert t0 == T // tm
    in_specs += [pl.BlockSpec((tm, D), lambda i: (i, gcol)),
                 pl.BlockSpec((tm, D), lambda i: (i, gcol + 1)),
                 row(D), layer(wa), layer(wb), layer(wo), full(g2), full(b2)]
    args += [proj, proj, x, wa, wb, wo, g2, b2]
    out_specs = [row(D), row(D)]
    out_shape = [jax.ShapeDtypeStruct((T, D), F32), jax.ShapeDtypeStruct((T, D), BF16)]
    if wr is not None:
        wr_hi = wr.astype(BF16)
        wr_lo = (wr - wr_hi.astype(F32)).astype(BF16)
        in_specs += [full(wr_hi), full(wr_lo)]
        args += [wr_hi, wr_lo]
        out_specs.append(row(LANES))
        out_shape.append(jax.ShapeDtypeStruct((T, LANES), F32))
    return pl.pallas_call(
        functools.partial(_post_kernel, alpha=alpha, with_router=wr is not None, tile_starts=tuple(tile_starts)),
        grid=(T // tm,),
        in_specs=in_specs,
        out_specs=out_specs,
        out_shape=out_shape,
        compiler_params=_cparams(("parallel",)),
    )(*args)


def _ffn_kernel(xb_ref, x_ref, wg_ref, wu_ref, wd_ref, g_ref, b_ref, xf_ref, xo_ref, acc_ref, *, alpha):
    f = pl.program_id(1)

    @pl.when(f == 0)
    def _():
        acc_ref[...] = jnp.zeros_like(acc_ref)

    xb = xb_ref[...]
    h = jax.nn.silu(_dot(xb, wg_ref[...])) * _dot(xb, wu_ref[...])
    acc_ref[...] += _dot(h.astype(BF16), wd_ref[...])

    @pl.when(f == pl.num_programs(1) - 1)
    def _():
        y = _layer_norm(alpha * x_ref[...] + acc_ref[...], g_ref[...], b_ref[...])
        xf_ref[...] = y
        xo_ref[...] = y.astype(BF16)


def _ffn(xb, x, wg, wu, wd, li, g, b, alpha, tm, tf):
    T, D = x.shape
    F = wg.shape[2]
    row = pl.BlockSpec((tm, D), lambda i, f: (i, 0))
    vec = pl.BlockSpec((1, D), lambda i, f: (0, 0))
    return pl.pallas_call(
        functools.partial(_ffn_kernel, alpha=alpha),
        grid=(T // tm, F // tf),
        in_specs=[row, row,
                  pl.BlockSpec((None, D, tf), lambda i, f: (li, 0, f)),
                  pl.BlockSpec((None, D, tf), lambda i, f: (li, 0, f)),
                  pl.BlockSpec((None, tf, D), lambda i, f: (li, f, 0)),
                  vec, vec],
        out_specs=[row, row],
        out_shape=[jax.ShapeDtypeStruct((T, D), F32), jax.ShapeDtypeStruct((T, D), BF16)],
        scratch_shapes=[pltpu.VMEM((tm, D), F32)],
        compiler_params=_cparams(("parallel", "arbitrary")),
    )(xb, x, wg, wu, wd, g.reshape(1, D), b.reshape(1, D))


def _route(logits, tm):
    T = logits.shape[0]
    top_logits, top_idx = lax.top_k(logits[:, :N_EXPERTS], TOP_K)
    gates = jax.nn.softmax(top_logits, axis=-1)
    n_assign = T * TOP_K
    flat_e = top_idx.reshape(n_assign).astype(jnp.int32)
    onehot = (flat_e[:, None] == jnp.arange(N_EXPERTS, dtype=jnp.int32)[None, :]).astype(jnp.int32)
    csum = jnp.cumsum(onehot, axis=0)
    counts = csum[-1]
    rank = jnp.sum((csum - onehot) * onehot, axis=1)
    padded = (counts + tm - 1) // tm * tm
    pad_end = jnp.cumsum(padded)
    pad_start = pad_end - padded
    dest = pad_start[flat_e] + rank
    n_blocks = -(-(n_assign + N_EXPERTS * (tm - 1)) // tm)
    n_rows = n_blocks * tm
    row_assign = jnp.full((n_rows,), -1, jnp.int32).at[dest].set(jnp.arange(n_assign, dtype=jnp.int32))
    used = row_assign >= 0
    row_src = jnp.where(used, row_assign // TOP_K, 0)
    spare = TOP_K * T + jnp.arange(n_rows, dtype=jnp.int32) % tm
    row_dst = jnp.where(used, (row_assign % TOP_K) * T + row_assign // TOP_K, spare)
    row_gate = jnp.where(used, gates.reshape(n_assign)[jnp.maximum(row_assign, 0)], 0.0)
    block_start = jnp.arange(n_blocks, dtype=jnp.int32) * tm
    block_expert = jnp.minimum(jnp.searchsorted(pad_end, block_start, side='right'), N_EXPERTS - 1).astype(jnp.int32)
    n_valid = (pad_end[-1] // tm).astype(jnp.int32).reshape(1)
    return block_expert, n_valid, row_src, row_dst, jnp.broadcast_to(row_gate[:, None], (n_rows, LANES))


def _moe_kernel(be_ref, nv_ref, src_ref, dst_ref, x_hbm, gate_ref, wg_ref, wu_ref, wd_ref, y_hbm,
                xg_ref, xb_ref, acc_ref, yo_ref, sem_in, sem_out, *, tm, chunk):
    i = pl.program_id(0)
    f = pl.program_id(1)
    n_f = pl.num_programs(1)
    n_blk = pl.num_programs(0)
    n_live = nv_ref[0]
    live = i < n_live
    slot = i % 2
    n_chunks = tm // chunk

    def gather_chunk(blk, buf, c):
        for r in range(c * chunk, (c + 1) * chunk):
            tok = src_ref[blk * tm + r]
            pltpu.make_async_copy(x_hbm.at[pl.ds(tok, 1)], xg_ref.at[buf, pl.ds(r, 1)], sem_in.at[buf]).start()

    def scatter_chunk(blk, c):
        for r in range(c * chunk, (c + 1) * chunk):
            dst = dst_ref[blk * tm + r]
            pltpu.make_async_copy(yo_ref.at[pl.ds(r, 1)], y_hbm.at[pl.ds(dst, 1)], sem_out).start(priority=r % 2)

    def wait_scatter():
        pltpu.make_async_copy(yo_ref, yo_ref, sem_out).wait()

    @pl.when((i == 0) & (f == 0))
    def _():
        for c in range(n_chunks):
            gather_chunk(0, 0, c)
        yo_ref[...] = jnp.zeros_like(yo_ref)
        spare = pltpu.make_async_copy(yo_ref, y_hbm.at[pl.ds(y_hbm.shape[0] - tm, tm)], sem_out)
        spare.start()
        spare.wait()

    @pl.when(live & (f == 0))
    def _():
        pltpu.make_async_copy(xg_ref.at[slot], xg_ref.at[slot], sem_in.at[slot]).wait()
        xb_ref[...] = xg_ref[slot].astype(BF16)
        acc_ref[...] = jnp.zeros_like(acc_ref)

    for c in range(n_chunks):
        @pl.when((f == c) & (i + 1 < n_live))
        def _():
            gather_chunk(i + 1, 1 - slot, c)

        @pl.when((f == c) & (i >= 1) & (i <= n_live))
        def _():
            scatter_chunk(i - 1, c)

    @pl.when(live)
    def _():
        xb = xb_ref[...]
        h = jax.nn.silu(_dot(xb, wg_ref[0])) * _dot(xb, wu_ref[0])
        acc_ref[...] += _dot(h.astype(BF16), wd_ref[0])

    @pl.when((f == n_f - 1) & (i >= 1) & (i <= n_live))
    def _():
        wait_scatter()

    @pl.when(live & (f == n_f - 1))
    def _():
        gate = gate_ref[...]
        for c in range(acc_ref.shape[1] // LANES):
            sl = slice(c * LANES, (c + 1) * LANES)
            yo_ref[:, sl] = acc_ref[:, sl] * gate

        @pl.when(i == n_blk - 1)
        def _():
            for c in range(n_chunks):
                scatter_chunk(i, c)
            wait_scatter()


def _moe(x, route, wg, wu, wd, li, tm, tf):
    T, D = x.shape
    F = wg.shape[2]
    nf = F // tf
    block_expert, n_valid, row_src, row_dst, row_gate = route
    n_blocks = block_expert.shape[0]
    chunk = next(c for c in range(8, tm + 1, 8) if tm % c == 0 and tm // c <= nf)

    def fidx(i, f, nv):
        return jnp.where(i < nv[0], f, nf - 1)

    grid_spec = pltpu.PrefetchScalarGridSpec(
        num_scalar_prefetch=4,
        grid=(n_blocks, nf),
        in_specs=[
            pl.BlockSpec(memory_space=pl.ANY),
            pl.BlockSpec((tm, LANES), lambda i, f, be, nv, rs, rd: (i, 0)),
            pl.BlockSpec((1, D, tf), lambda i, f, be, nv, rs, rd: (li * N_EXPERTS + be[i], 0, fidx(i, f, nv))),
            pl.BlockSpec((1, D, tf), lambda i, f, be, nv, rs, rd: (li * N_EXPERTS + be[i], 0, fidx(i, f, nv))),
            pl.BlockSpec((1, tf, D), lambda i, f, be, nv, rs, rd: (li * N_EXPERTS + be[i], fidx(i, f, nv), 0)),
        ],
        out_specs=pl.BlockSpec(memory_space=pl.ANY),
        scratch_shapes=[pltpu.VMEM((2, tm, D), F32), pltpu.VMEM((tm, D), BF16), pltpu.VMEM((tm, D), F32),
                        pltpu.VMEM((tm, D), F32), pltpu.SemaphoreType.DMA((2,)), pltpu.SemaphoreType.DMA(())],
    )
    y = pl.pallas_call(
        functools.partial(_moe_kernel, tm=tm, chunk=chunk),
        grid_spec=grid_spec,
        out_shape=jax.ShapeDtypeStruct((TOP_K * T + tm, D), F32),
        compiler_params=_cparams(("arbitrary", "arbitrary")),
    )(block_expert, n_valid, row_src, row_dst, x, row_gate, wg, wu, wd)
    return y


def _combine_kernel(x_ref, y0_ref, y1_ref, g_ref, b_ref, xf_ref, *rest, alpha):
    y = _layer_norm(alpha * x_ref[...] + (y0_ref[...] + y1_ref[...]), g_ref[...], b_ref[...])
    xf_ref[...] = y
    for xb_ref in rest:
        xb_ref[...] = y.astype(BF16)


def _combine(x, y, g, b, alpha, tm, row0, n_rows, with_bf16):
    T, D = x.shape
    n = T // tm
    i0 = row0 // tm
    row = pl.BlockSpec((tm, D), lambda i: (i, 0))
    vec = pl.BlockSpec((1, D), lambda i: (0, 0))
    out_specs = [row, row] if with_bf16 else [row]
    out_shape = [jax.ShapeDtypeStruct((n_rows, D), F32)]
    if with_bf16:
        out_shape.append(jax.ShapeDtypeStruct((n_rows, D), BF16))
    return pl.pallas_call(
        functools.partial(_combine_kernel, alpha=alpha),
        grid=(n_rows // tm,),
        in_specs=[pl.BlockSpec((tm, D), lambda i: (i0 + i, 0)),
                  pl.BlockSpec((tm, D), lambda i: (i0 + i, 0)),
                  pl.BlockSpec((tm, D), lambda i: (n + i0 + i, 0)), vec, vec],
        out_specs=out_specs,
        out_shape=out_shape,
        compiler_params=_cparams(("parallel",)),
    )(x, y, y, g.reshape(1, D), b.reshape(1, D))


def _tiles(D):
    big = D >= 2048
    return dict(ln=512, proj=2048, post=256 if big else 512, ffn_m=512, ffn_f=512,
                moe_m=512, moe_f=1024, comb=512)


def kernel(x_prompt, x_sample, ln_in_g, ln_in_b, w_in, w_branch_a, w_branch_b, w_out, diff_lambda, diff_subln_g, ln_mix_g, ln_mix_b, ln_ffn_g, ln_ffn_b, ffn_w_gate, ffn_w_up, ffn_w_down, moe_router, moe_w_gate, moe_w_up, moe_w_down):
    depth = w_in.shape[0]
    D = x_prompt.shape[-1]
    alpha = (2 * depth) ** 0.25
    tl = _tiles(D)
    seqs = [(x_prompt.shape[0], x_prompt.shape[1]), (x_sample.shape[0], x_sample.shape[1])]
    T = sum(n_seq * S for n_seq, S in seqs)
    tab = _rope_tables(max(S for _, S in seqs))
    bf = lambda w: w.astype(BF16)
    w_in, w_branch_a, w_branch_b, w_out = bf(w_in), bf(w_branch_a), bf(w_branch_b), bf(w_out)
    ffn_w_gate, ffn_w_up, ffn_w_down = bf(ffn_w_gate), bf(ffn_w_up), bf(ffn_w_down)
    merge = lambda w: bf(w).reshape((-1,) + w.shape[2:])
    moe_w_gate, moe_w_up, moe_w_down = merge(moe_w_gate), merge(moe_w_up), merge(moe_w_down)

    xf, xb = _ln_in([x_prompt.reshape(-1, D), x_sample.reshape(-1, D)], ln_in_g, ln_in_b, tl['ln'])
    for l in range(depth):
        lam_init = 0.8 - 0.6 * math.exp(-0.3 * l)
        consts = jnp.full((1, LANES), lam_init, F32)
        proj = _proj(xb, w_in, l, tab, seqs, tl['proj'])

        att, row0 = [], 0
        for n_seq, S in seqs:
            groups = [_attn_a_group(proj, g, n_seq, S, row0) for g in range(len(A_GROUPS))]
            ob = _attn_b(proj, diff_lambda[l], diff_subln_g[l], consts, n_seq, S, row0)
            att.append([o for o, _ in groups] + [lse for _, lse in groups] + [ob])
            row0 += n_seq * S

        moe_layer = l % 2 == 1
        i = l // 2
        wr = None
        if moe_layer:
            wr = jnp.zeros((D, LANES), F32).at[:, :N_EXPERTS].set(moe_router[i])
        res = _post(att, proj, xf, w_branch_a, w_branch_b, w_out, l,
                    ln_mix_g[l], ln_mix_b[l], wr, alpha, tl['post'])
        if moe_layer:
            xf, xb, logits = res
            route = _route(logits, tl['moe_m'])
            y = _moe(xf, route, moe_w_gate, moe_w_up, moe_w_down, i, tl['moe_m'], tl['moe_f'])
            if l == depth - 1:
                n_p = seqs[0][0] * seqs[0][1]
                outs = [_combine(xf, y, ln_ffn_g[l], ln_ffn_b[l], alpha, tl['comb'], r0, n, False)[0]
                        for r0, n in ((0, n_p), (n_p, T - n_p))]
                return (outs[0].reshape(x_prompt.shape), outs[1].reshape(x_sample.shape))
            xf, xb = _combine(xf, y, ln_ffn_g[l], ln_ffn_b[l], alpha, tl['comb'], 0, T, True)
        else:
            xf, xb = res
            xf, xb = _ffn(xb, xf, ffn_w_gate, ffn_w_up, ffn_w_down, i,
                          ln_ffn_g[l], ln_ffn_b[l], alpha, tl['ffn_m'], tl['ffn_f'])

    n_p = seqs[0][0] * seqs[0][1]
    return (xf[:n_p].reshape(x_prompt.shape), xf[n_p:].reshape(x_sample.shape))
```
